```python
import math
import jax, jax.numpy as jnp
from jax import lax
import numpy as np

D_MODEL = 2048
BATCH = 1
SEQ = 16384
DEPTH = 4

CHUNK = 64
Q_BLOCK = 128
MIX_WIDTH = D_MODEL
DIFF_WIDTH = MIX_WIDTH // 2
RWKV_WIDTH = MIX_WIDTH - DIFF_WIDTH
DIFF_V_DIM = 128
DIFF_QK_DIM = DIFF_V_DIM // 2
DIFF_HEADS = DIFF_WIDTH // DIFF_V_DIM
DIFF_QK_W = DIFF_HEADS * 2 * DIFF_QK_DIM
DIFF_V_W = DIFF_HEADS * DIFF_V_DIM
RWKV_HEAD_DIM = 64
RWKV_HEADS = RWKV_WIDTH // RWKV_HEAD_DIM
DECAY_LORA = 64
AAA_LORA = 64
MV_LORA = 32
GATE_LORA = 160
RWKV_SHIFT_WIDTH = 3 * RWKV_WIDTH + DECAY_LORA + AAA_LORA + GATE_LORA
PROJ_WIDTH = 2 * DIFF_QK_W + DIFF_V_W + RWKV_SHIFT_WIDTH
D_FF = 4 * D_MODEL
RMS_EPS = 1e-6
RWKV_GN_EPS = 64e-5
NEG_INF = -1e30

kernel_name = "hybrid_diffattn_rwkv7_sqrelu_trunk"


def rms_norm(x, g, eps=RMS_EPS):
    xf = x.astype(jnp.float32)
    y = xf * lax.rsqrt(jnp.mean(xf * xf, axis=-1, keepdims=True) + eps)
    return (y * g.astype(jnp.float32)).astype(x.dtype)


def alibi_slopes(n):
    return jnp.asarray([2.0 ** (-8.0 * (i + 1) / n) for i in range(n)], jnp.float32)


def token_shift(p, mu):
    prev = jnp.pad(p, ((0, 0), (1, 0), (0, 0)))[:, :-1]
    return p + (prev - p) * mu


def diff_attention(q, k, v, lam, slopes):
    B, T, H, _, DK = q.shape
    nb = T // Q_BLOCK
    scale = 1.0 / math.sqrt(DK)
    q_blocks = jnp.moveaxis(q.reshape(B, nb, Q_BLOCK, H, 2, DK), 1, 0)
    k_pos = jnp.arange(T)

    def one_block(args):
        q_blk, blk = args
        q_pos = blk * Q_BLOCK + jnp.arange(Q_BLOCK)
        s = jnp.einsum('bqhcd,bkhcd->bhcqk', q_blk, k) * scale
        dist = jnp.abs(q_pos[:, None] - k_pos[None, :]).astype(jnp.float32)
        allowed = (k_pos[None, :] // CHUNK) <= (q_pos[:, None] // CHUNK)
        s = s - slopes[None, :, None, None, None] * dist
        s = jnp.where(allowed, s, NEG_INF)
        p = jax.nn.softmax(s, axis=-1)
        attn = p[:, :, 0] - lam * p[:, :, 1]
        return jnp.einsum('bhqk,bkhd->bqhd', attn, v)

    out = lax.map(one_block, (q_blocks, jnp.arange(nb)))
    return jnp.moveaxis(out, 0, 1).reshape(B, T, H, v.shape[-1])


def rwkv7_scan(r, w, k, v, a, b):
    B, T, H, N = r.shape

    def step(S, inp):
        r_t, w_t, k_t, v_t, a_t, b_t = inp
        sa = jnp.einsum('bhij,bhj->bhi', S, a_t)
        S = S * w_t[:, :, None, :] + sa[..., None] * b_t[:, :, None, :] + v_t[..., None] * k_t[:, :, None, :]
        y = jnp.einsum('bhij,bhj->bhi', S, r_t)
        return S, y

    xs = tuple(jnp.moveaxis(t, 1, 0) for t in (r, w, k, v, a, b))
    S0 = jnp.zeros((B, H, N, N), jnp.float32)
    _, ys = lax.scan(step, S0, xs)
    return jnp.moveaxis(ys, 0, 1)


def setup_inputs(seed: int = 0) -> dict:
    key = jax.random.key(seed)
    ks = jax.random.split(key, 26)
    f = jnp.float32
    L = DEPTH

    def nrm(k, shape, scale):
        return jax.random.normal(k, shape, f) * scale

    return {
        "x": nrm(ks[0], (BATCH, SEQ, D_MODEL), 1.0),
        "norm_mix": 1.0 + nrm(ks[1], (L, D_MODEL), 0.02),
        "norm_mlp": 1.0 + nrm(ks[2], (L, D_MODEL), 0.02),
        "w_in": nrm(ks[3], (L, D_MODEL, PROJ_WIDTH), D_MODEL ** -0.5),
        "w_out": nrm(ks[4], (L, MIX_WIDTH, D_MODEL), MIX_WIDTH ** -0.5),
        "qk_norm_q": 1.0 + nrm(ks[5], (L, 2, DIFF_QK_DIM), 0.02),
        "qk_norm_k": 1.0 + nrm(ks[6], (L, 2, DIFF_QK_DIM), 0.02),
        "diff_lambda_q": nrm(ks[7], (L, 2, DIFF_QK_DIM), 0.1),
        "diff_lambda_k": nrm(ks[8], (L, 2, DIFF_QK_DIM), 0.1),
        "diff_subln": 1.0 + nrm(ks[9], (L, DIFF_V_DIM), 0.02),
        "rwkv_mu": jax.random.uniform(ks[10], (L, RWKV_SHIFT_WIDTH), f, 0.0, 1.0),
        "rwkv_w0": -0.5 + nrm(ks[11], (L, RWKV_WIDTH), 0.5),
        "rwkv_w2": nrm(ks[12], (L, DECAY_LORA, RWKV_WIDTH), 0.5 * DECAY_LORA ** -0.5),
        "rwkv_a0": nrm(ks[13], (L, RWKV_WIDTH), 0.1),
        "rwkv_a2": nrm(ks[14], (L, AAA_LORA, RWKV_WIDTH), 0.5 * AAA_LORA ** -0.5),
        "rwkv_g2": nrm(ks[15], (L, GATE_LORA, RWKV_WIDTH), GATE_LORA ** -0.5),
        "rwkv_k_k": 0.85 + nrm(ks[16], (L, RWKV_WIDTH), 0.05),
        "rwkv_k_a": 1.0 + nrm(ks[17], (L, RWKV_WIDTH), 0.05),
        "rwkv_r_k": nrm(ks[18], (L, RWKV_HEADS, RWKV_HEAD_DIM), 0.1),
        "rwkv_ln_w": 1.0 + nrm(ks[19], (L, RWKV_WIDTH), 0.02),
        "rwkv_ln_b": nrm(ks[20], (L, RWKV_WIDTH), 0.02),
        "rwkv_v0": nrm(ks[21], (L - 1, RWKV_WIDTH), 0.5),
        "rwkv_v1": nrm(ks[22], (L - 1, RWKV_WIDTH, MV_LORA), RWKV_WIDTH ** -0.5),
        "rwkv_v2": nrm(ks[23], (L - 1, MV_LORA, RWKV_WIDTH), 0.5 * MV_LORA ** -0.5),
        "mlp_up": nrm(ks[24], (L, D_MODEL, D_FF), D_MODEL ** -0.5),
        "mlp_down": nrm(ks[25], (L, D_FF, D_MODEL), D_FF ** -0.5),
    }


def reference(x, norm_mix, norm_mlp, w_in, w_out, qk_norm_q, qk_norm_k, diff_lambda_q,
              diff_lambda_k, diff_subln, rwkv_mu, rwkv_w0, rwkv_w2, rwkv_a0, rwkv_a2, rwkv_g2,
              rwkv_k_k, rwkv_k_a, rwkv_r_k, rwkv_ln_w, rwkv_ln_b, rwkv_v0, rwkv_v1, rwkv_v2,
              mlp_up, mlp_down):
    B, T, _ = x.shape
    f32 = jnp.float32
    slopes = alibi_slopes(DIFF_HEADS)
    v_first = None
    for l in range(DEPTH):
        h = rms_norm(x, norm_mix[l])
        proj = jnp.einsum('btd,de->bte', h, w_in[l])
        q_d, k_d, v_d, rw = jnp.split(
            proj, [DIFF_QK_W, 2 * DIFF_QK_W, 2 * DIFF_QK_W + DIFF_V_W], axis=-1)

        q_d = rms_norm(q_d.reshape(B, T, DIFF_HEADS, 2, DIFF_QK_DIM), qk_norm_q[l]).astype(f32)
        k_d = rms_norm(k_d.reshape(B, T, DIFF_HEADS, 2, DIFF_QK_DIM), qk_norm_k[l]).astype(f32)
        v_d = v_d.reshape(B, T, DIFF_HEADS, DIFF_V_DIM).astype(f32)
        lam_init = 0.8 - 0.6 * math.exp(-0.3 * l)
        lq = diff_lambda_q[l].astype(f32)
        lk = diff_lambda_k[l].astype(f32)
        lam = jnp.exp(jnp.sum(lq[0] * lk[0])) - jnp.exp(jnp.sum(lq[1] * lk[1])) + lam_init
        a_out = diff_attention(q_d, k_d, v_d, lam, slopes)
        a_out = rms_norm(a_out, diff_subln[l]) * (1.0 - lam_init)
        a_out = a_out.reshape(B, T, DIFF_WIDTH).astype(x.dtype)

        rw = token_shift(rw, rwkv_mu[l])
        r, kr, vr, lat_w, lat_a, lat_g = jnp.split(
            rw, [RWKV_WIDTH, 2 * RWKV_WIDTH, 3 * RWKV_WIDTH,
                 3 * RWKV_WIDTH + DECAY_LORA, 3 * RWKV_WIDTH + DECAY_LORA + AAA_LORA], axis=-1)
        w_log = -jax.nn.softplus(-(rwkv_w0[l] + jnp.tanh(lat_w) @ rwkv_w2[l])) - 0.5
        decay = jnp.exp(-jnp.exp(w_log.astype(f32)))
        if l == 0:
            v_first = vr
        else:
            v_gate = jax.nn.sigmoid(rwkv_v0[l - 1] + (vr @ rwkv_v1[l - 1]) @ rwkv_v2[l - 1])
            vr = vr + (v_first - vr) * v_gate
        a_rate = jax.nn.sigmoid(rwkv_a0[l] + lat_a @ rwkv_a2[l])
        g = jax.nn.sigmoid(lat_g) @ rwkv_g2[l]
        heads = (B, T, RWKV_HEADS, RWKV_HEAD_DIM)
        kk = (kr * rwkv_k_k[l]).astype(f32).reshape(heads)
        kk = kk / jnp.maximum(jnp.sqrt(jnp.sum(kk * kk, axis=-1, keepdims=True)), 1e-12)
        a_h = a_rate.astype(f32).reshape(heads)
        k_mod = (kr * (1.0 + (a_rate - 1.0) * rwkv_k_a[l])).astype(f32).reshape(heads)
        r_h = r.astype(f32).reshape(heads)
        v_h = vr.astype(f32).reshape(heads)
        y = rwkv7_scan(r_h, decay.reshape(heads), k_mod, v_h, -kk, kk * a_h)
        mean = jnp.mean(y, axis=-1, keepdims=True)
        var = jnp.mean(jnp.square(y - mean), axis=-1, keepdims=True)
        y = ((y - mean) * lax.rsqrt(var + RWKV_GN_EPS)).reshape(B, T, RWKV_WIDTH)
        y = y * rwkv_ln_w[l].astype(f32) + rwkv_ln_b[l].astype(f32)
        bonus = jnp.sum(r_h * k_mod * rwkv_r_k[l].astype(f32), axis=-1, keepdims=True) * v_h
        y = (y + bonus.reshape(B, T, RWKV_WIDTH)).astype(x.dtype) * g

        mixed = jnp.concatenate([a_out, y], axis=-1)
        x = x + jnp.einsum('bte,ed->btd', mixed, w_out[l])

        h2 = rms_norm(x, norm_mlp[l])
        u = jnp.square(jax.nn.relu(jnp.einsum('btd,df->btf', h2, mlp_up[l])))
        x = x + jnp.einsum('btf,fd->btd', u, mlp_down[l])
    return x
```

```python
import functools
import math

import jax
import jax.numpy as jnp
from jax import lax
from jax.experimental import pallas as pl
from jax.experimental.pallas import tpu as pltpu

F32 = jnp.float32
BF16 = jnp.bfloat16

D_MODEL = 2048
DEPTH = 4
CHUNK = 64
N_HEADS_A = 8
HEAD_A = 128
MAP_DIM = 64
ATT_W = N_HEADS_A * HEAD_A
RW_W = 1024
RW_HEAD = 64
DECAY_LORA = 64
AAA_LORA = 64
GATE_LORA = 160
LAT_W = DECAY_LORA + AAA_LORA + GATE_LORA
LAT_PAD = 384
D_FF = 4 * D_MODEL
RMS_EPS = 1e-6
RWKV_GN_EPS = 64e-5
NEG_BIG = -1e30
LOG2E = 1.4426950408889634

LANES = 128
QUAD = 4 * RW_HEAD
RCHUNK = 64
SUB = 16
VMEM_LIMIT = 56 * 1024 * 1024

HI = lax.Precision.HIGHEST


def _cparams(sem):
    return pltpu.CompilerParams(dimension_semantics=sem, vmem_limit_bytes=VMEM_LIMIT)


def _dot(a, b):
    return jnp.dot(a.astype(BF16), b.astype(BF16), preferred_element_type=F32)


def _dot_nt(a, b):
    return lax.dot_general(a.astype(BF16), b.astype(BF16), (((1,), (1,)), ((), ())),
                           preferred_element_type=F32)


def _dot_tn(a, b):
    return lax.dot_general(a.astype(BF16), b.astype(BF16), (((0,), (0,)), ((), ())),
                           preferred_element_type=F32)


def _dot_hi(a, b):
    return jnp.dot(a, b, precision=HI, preferred_element_type=F32)


def _group_ones(n, group):
    r = lax.broadcasted_iota(jnp.int32, (n, n), 0) // group
    c = lax.broadcasted_iota(jnp.int32, (n, n), 1) // group
    return (r == c).astype(F32)


def _proj_kernel(x_ref, g_ref, w_ref, *rest, qk):
    if qk:
        gain_ref, o_ref, h_ref = rest
    else:
        o_ref, h_ref = rest

    @pl.when(pl.program_id(1) == 0)
    def _():
        x = x_ref[...]
        ms = jnp.mean(x * x, axis=-1, keepdims=True)
        h_ref[...] = (x * lax.rsqrt(ms + RMS_EPS) * g_ref[...]).astype(BF16)

    acc = jnp.dot(h_ref[...], w_ref[...], preferred_element_type=F32)
    if not qk:
        o_ref[...] = acc.astype(o_ref.dtype)
        return
    tn = acc.shape[1]
    ones = _group_ones(LANES, MAP_DIM).astype(BF16)
    lane = lax.broadcasted_iota(jnp.int32, (1, LANES), 1)
    first = lane < MAP_DIM
    for s in range(tn // LANES):
        y = acc[:, s * LANES:(s + 1) * LANES]
        ss = jnp.dot((y * y).astype(BF16), ones, preferred_element_type=F32)
        yn = y * lax.rsqrt(ss * (1.0 / MAP_DIM) + RMS_EPS) * gain_ref[:, s * LANES:(s + 1) * LANES]
        o_ref[0, :, s * LANES:(s + 1) * LANES] = jnp.where(first, yn, 0.0).astype(o_ref.dtype)
        o_ref[1, :, s * LANES:(s + 1) * LANES] = jnp.where(
            first, pltpu.roll(yn, MAP_DIM, axis=1), 0.0).astype(o_ref.dtype)


def _proj(x, g, w, out_dtype, gain=None, tm=1024, tn=512):
    T, D = x.shape
    N = w.shape[1]
    tm = min(tm, T)
    tn = min(tn, N)
    if N % tn:
        tn = N
    qk = gain is not None
    in_specs = [pl.BlockSpec((tm, D), lambda i, j: (i, 0)),
                pl.BlockSpec((1, D), lambda i, j: (0, 0)),
                pl.BlockSpec((D, tn), lambda i, j: (0, j))]
    args = [x, g.reshape(1, D), w]
    if qk:
        in_specs.append(pl.BlockSpec((1, tn), lambda i, j: (0, j)))
        args.append(gain.reshape(1, N))
        out_shape = jax.ShapeDtypeStruct((2, T, N), out_dtype)
        out_spec = pl.BlockSpec((2, tm, tn), lambda i, j: (0, i, j))
    else:
        out_shape = jax.ShapeDtypeStruct((T, N), out_dtype)
        out_spec = pl.BlockSpec((tm, tn), lambda i, j: (i, j))
    return pl.pallas_call(
        functools.partial(_proj_kernel, qk=qk),
        grid=(T // tm, N // tn),
        in_specs=in_specs,
        out_specs=out_spec,
        out_shape=out_shape,
        scratch_shapes=[pltpu.VMEM((tm, D), BF16)],
        compiler_params=_cparams(("parallel", "arbitrary")),
        name="proj_qk" if qk else "proj",
    )(*args)


def _attn_kernel(q_ref, k_ref, v_ref, hp_ref, o_ref, m_ref, l_ref, acc_ref, *, bq):
    i = pl.program_id(1)
    c = hp_ref[0, 0:1, 0:1]
    lam = hp_ref[0, 1:2, :]
    sub = hp_ref[0, 2:3, :]

    m_ref[...] = jnp.full(m_ref.shape, NEG_BIG, F32)
    l_ref[...] = jnp.zeros(l_ref.shape, F32)
    acc_ref[...] = jnp.zeros(acc_ref.shape, F32)

    kj = lax.broadcasted_iota(jnp.int32, (1, bq), 1).astype(F32)
    ckj = c * kj

    def update(mp, s, v):
        m_old = m_ref[mp]
        m_new = jnp.maximum(m_old, jnp.max(s, axis=-1, keepdims=True))
        alpha = jnp.exp2(m_old - m_new)
        p = jnp.exp2(s - m_new)
        l_ref[mp] = alpha * l_ref[mp] + jnp.sum(p, axis=-1, keepdims=True)
        acc_ref[mp] = alpha * acc_ref[mp] + jnp.dot(p.astype(BF16), v, preferred_element_type=F32)
        m_ref[mp] = m_new

    def body(j, carry):
        start = pl.multiple_of(j * bq, bq)
        v = v_ref[pl.ds(start, bq), :]
        bias = ckj - c * ((i - j) * bq).astype(F32)
        for mp in range(2):
            s = _dot_nt(q_ref[mp], k_ref[mp, pl.ds(start, bq), :]) + bias
            update(mp, s, v)
        return carry

    lax.fori_loop(0, i, body, 0)

    start = pl.multiple_of(i * bq, bq)
    v = v_ref[pl.ds(start, bq), :]
    qi = lax.broadcasted_iota(jnp.int32, (bq, bq), 0)
    kk = lax.broadcasted_iota(jnp.int32, (bq, bq), 1)
    allowed = (kk // CHUNK) <= (qi // CHUNK)
    dbias = c * jnp.minimum(kk, 2 * qi - kk).astype(F32)
    for mp in range(2):
        s = _dot_nt(q_ref[mp], k_ref[mp, pl.ds(start, bq), :]) + dbias
        update(mp, jnp.where(allowed, s, NEG_BIG), v)

    o = acc_ref[0] / l_ref[0] - lam * (acc_ref[1] / l_ref[1])
    ms = jnp.mean(o * o, axis=-1, keepdims=True)
    o_ref[...] = (o * lax.rsqrt(ms + RMS_EPS) * sub).astype(o_ref.dtype)


ATT_BQ = 512


def _attention(qk, v, hp):
    T = v.shape[0]
    bq = min(ATT_BQ, T)
    return pl.pallas_call(
        functools.partial(_attn_kernel, bq=bq),
        grid=(N_HEADS_A, T // bq),
        in_specs=[pl.BlockSpec((2, bq, HEAD_A), lambda h, i: (0, i, h)),
                  pl.BlockSpec((2, T, HEAD_A), lambda h, i: (0, 0, N_HEADS_A + h)),
                  pl.BlockSpec((T, HEAD_A), lambda h, i: (0, h)),
                  pl.BlockSpec((1, 8, HEAD_A), lambda h, i: (h, 0, 0))],
        out_specs=pl.BlockSpec((bq, HEAD_A), lambda h, i: (i, h)),
        out_shape=jax.ShapeDtypeStruct((T, ATT_W), BF16),
        scratch_shapes=[pltpu.VMEM((2, bq, 1), F32), pltpu.VMEM((2, bq, 1), F32),
                        pltpu.VMEM((2, bq, HEAD_A), F32)],
        compiler_params=_cparams(("parallel", "arbitrary")),
        name="diff_attn",
    )(qk, qk, v, hp)


def _shift(p, carry_row, mu):
    rolled = pltpu.roll(p, 1, axis=0)
    row = lax.broadcasted_iota(jnp.int32, p.shape, 0)
    prev = jnp.where(row == 0, carry_row, rolled)
    return p + (prev - p) * mu


def _prep_kernel(rkv_ref, lat_ref, vf_ref, mu_ref, mul_ref, vec_ref, w2_ref, a2_ref, g2_ref,
                 v1_ref, v2_ref,
                 r_ref, lw_ref, k_ref, v_ref, a_ref, b_ref, g_ref,
                 c_rkv, c_lat, *, has_gate):
    @pl.when(pl.program_id(0) == 0)
    def _():
        c_rkv[...] = jnp.zeros(c_rkv.shape, F32)
        c_lat[...] = jnp.zeros(c_lat.shape, F32)

    p = rkv_ref[...]
    lt = lat_ref[...]
    n = p.shape[0]
    ps = _shift(p, c_rkv[0:1, :], mu_ref[...])
    ls = _shift(lt, c_lat[0:1, :], mul_ref[...])
    c_rkv[0:1, :] = p[n - 1:n, :]
    c_lat[0:1, :] = lt[n - 1:n, :]

    w0, a0, k_k, k_a, v0 = (vec_ref[i:i + 1, :] for i in range(5))
    r = ps[:, 0:RW_W]
    kr = ps[:, RW_W:2 * RW_W]
    vr = ps[:, 2 * RW_W:3 * RW_W]
    lat_w = ls[:, 0:DECAY_LORA]
    lat_a = ls[:, DECAY_LORA:DECAY_LORA + AAA_LORA]
    lat_g = ls[:, DECAY_LORA + AAA_LORA:LAT_W]

    xw = w0 + _dot_hi(jnp.tanh(lat_w), w2_ref[...])
    w_log = -jax.nn.softplus(-xw) - 0.5
    lw_ref[...] = -jnp.exp(w_log)
    if has_gate:
        gate = jax.nn.sigmoid(v0 + _dot_hi(_dot_hi(vr, v1_ref[...]), v2_ref[...]))
        vr = vr + (vf_ref[...] - vr) * gate
    a_rate = jax.nn.sigmoid(a0 + _dot_hi(lat_a, a2_ref[...]))
    g_ref[...] = _dot_hi(jax.nn.sigmoid(lat_g), g2_ref[...])
    kk = kr * k_k
    ones = _group_ones(LANES, RW_HEAD)
    for s in range(RW_W // LANES):
        sl = slice(s * LANES, (s + 1) * LANES)
        ks = kk[:, sl]
        nrm = jnp.sqrt(_dot_hi(ks * ks, ones))
        kn = ks / jnp.maximum(nrm, 1e-12)
        a_ref[:, sl] = -kn
        b_ref[:, sl] = kn * a_rate[:, sl]
    r_ref[...] = r
    k_ref[...] = kr * (1.0 + (a_rate - 1.0) * k_a)
    v_ref[...] = vr


def _rwkv_prep(rkv, lat, v_first, mu_rkv, mu_lat, vecs, w2, a2, g2, v1, v2, has_gate, tp=256):
    T = rkv.shape[0]
    tp = min(tp, T)
    row = lambda w: pl.BlockSpec((tp, w), lambda i: (i, 0))
    full = lambda a: pl.BlockSpec(a.shape, lambda i: (0,) * a.ndim)
    outs = [jax.ShapeDtypeStruct((T, RW_W), F32)] * 7
    return pl.pallas_call(
        functools.partial(_prep_kernel, has_gate=has_gate),
        grid=(T // tp,),
        in_specs=[row(3 * RW_W), row(LAT_PAD), row(RW_W), full(mu_rkv), full(mu_lat), full(vecs),
                  full(w2), full(a2), full(g2), full(v1), full(v2)],
        out_specs=[row(RW_W)] * 7,
        out_shape=outs,
        scratch_shapes=[pltpu.VMEM((8, 3 * RW_W), F32), pltpu.VMEM((8, LAT_PAD), F32)],
        compiler_params=_cparams(("arbitrary",)),
        name="rwkv_prep",
    )(rkv, lat, v_first, mu_rkv, mu_lat, vecs, w2, a2, g2, v1, v2)


def _bdiag(y, masks):
    return jnp.concatenate([y * m for m in masks], axis=0)


def _scan_kernel(r_ref, lw_ref, k_ref, v_ref, a_ref, b_ref, g_ref, par_ref, o_ref, s_ref, *, n_chunks):
    C = RCHUNK

    @pl.when(pl.program_id(0) == 0)
    def _():
        s_ref[...] = jnp.zeros(s_ref.shape, F32)

    lane = lax.broadcasted_iota(jnp.int32, (1, QUAD), 1) // RW_HEAD
    hm1 = [(lane == q).astype(F32) for q in range(4)]
    hm2 = [jnp.concatenate([m, m], axis=1) for m in hm1]
    hm3 = [jnp.concatenate([m, m, m], axis=1) for m in hm1]
    t_i = lax.broadcasted_iota(jnp.int32, (C, QUAD), 0)
    s_i = lax.broadcasted_iota(jnp.int32, (C, QUAD), 1) % C
    strict = (s_i < t_i).astype(F32)
    incl = (s_i <= t_i).astype(F32)
    same_sub = ((s_i // SUB) == (t_i // SUB)).astype(F32)
    eye_q = (s_i == t_i).astype(F32)
    bd = _group_ones(QUAD, RW_HEAD)
    eye = (lax.broadcasted_iota(jnp.int32, (QUAD, QUAD), 0)
           == lax.broadcasted_iota(jnp.int32, (QUAD, QUAD), 1)).astype(F32)
    tri = (lax.broadcasted_iota(jnp.int32, (C, C), 1)
           <= lax.broadcasted_iota(jnp.int32, (C, C), 0)).astype(F32)
    gavg = bd * (1.0 / RW_HEAD)

    def quad_body(qd, carry):
        col = pl.multiple_of(qd * QUAD, QUAD)
        cs = pl.ds(col, QUAD)
        ln_w = par_ref[0:1, cs]
        ln_b = par_ref[1:2, cs]
        r_k = par_ref[2:3, cs]

        def chunk_body(ci, S):
            rs = pl.ds(pl.multiple_of(ci * C, C), C)
            r = r_ref[rs, cs]
            lw = lw_ref[rs, cs]
            k = k_ref[rs, cs]
            v = v_ref[rs, cs]
            a = a_ref[rs, cs]
            b = b_ref[rs, cs]
            cw = _dot_hi(tri, lw)
            e_pos = jnp.exp(cw)
            e_neg = jnp.exp(-cw)
            at = a * jnp.exp(cw - lw)
            rt = r * e_pos
            bt = b * e_neg
            kt = k * e_neg
            wc = e_pos[C - 1:C, :]

            z = jnp.concatenate([bt * m for m in hm1] + [kt * m for m in hm1], axis=0)
            g4 = _dot_nt(jnp.concatenate([at, rt], axis=0), z)
            a_ab = g4[0:C, 0:QUAD] * strict
            a_ak = g4[0:C, QUAD:2 * QUAD] * strict
            a_rb = g4[C:2 * C, 0:QUAD] * incl
            a_rk = g4[C:2 * C, QUAD:2 * QUAD] * incl

            d1 = a_ab * same_sub
            a_off = a_ab - d1
            d2 = _dot(d1, _bdiag(d1, hm1))
            d4 = _dot(d2, _bdiag(d2, hm1))
            d8 = _dot(d4, _bdiag(d4, hm1))
            p = eye_q + d1
            p = p + _dot(p, _bdiag(d2, hm1))
            p = p + _dot(p, _bdiag(d4, hm1))
            t_d = p + _dot(p, _bdiag(d8, hm1))

            akv = _dot(a_ak, _bdiag(v, hm1))
            nx = _dot(t_d, _bdiag(jnp.concatenate([a_off, at, akv], axis=1), hm3))
            n_m = nx[:, 0:QUAD]
            cc = nx[:, QUAD:3 * QUAD]
            u = cc
            for _ in range(C // SUB - 1):
                u = cc + _dot(n_m, _bdiag(u, hm2))
            a_hat = u[:, 0:QUAD]
            u_hat = u[:, QUAD:2 * QUAD]

            m_c = (eye + _dot_tn(a_hat, bt) * bd) * wc
            n_c = (_dot_tn(u_hat, bt) + _dot_tn(v, kt)) * bd * wc
            q = rt + _dot(a_rb, _bdiag(a_hat, hm1))
            y0 = _dot(jnp.concatenate([a_rb, a_rk], axis=1),
                      jnp.concatenate([_bdiag(u_hat, hm1), _bdiag(v, hm1)], axis=0))
            y = _dot_nt(q, S) + y0
            S_new = _dot_hi(S, m_c) + n_c

            mean = _dot_hi(y, gavg)
            dlt = y - mean
            var = _dot_hi(dlt * dlt, gavg)
            yn = dlt * lax.rsqrt(var + RWKV_GN_EPS) * ln_w + ln_b
            bonus = _dot_hi(r * k * r_k, bd) * v
            o_ref[rs, cs] = ((yn + bonus) * g_ref[rs, cs]).astype(o_ref.dtype)
            return S_new

        s_ref[qd] = lax.fori_loop(0, n_chunks, chunk_body, s_ref[qd])
        return carry

    lax.fori_loop(0, RW_W // QUAD, quad_body, 0)


def _rwkv_scan(r, lw, k, v, a, b, g, par, ls=256):
    T = r.shape[0]
    ls = min(ls, T)
    row = pl.BlockSpec((ls, RW_W), lambda i: (i, 0))
    return pl.pallas_call(
        functools.partial(_scan_kernel, n_chunks=ls // RCHUNK),
        grid=(T // ls,),
        in_specs=[row] * 7 + [pl.BlockSpec(par.shape, lambda i: (0, 0))],
        out_specs=row,
        out_shape=jax.ShapeDtypeStruct((T, RW_W), BF16),
        scratch_shapes=[pltpu.VMEM((RW_W // QUAD, QUAD, QUAD), F32)],
        compiler_params=_cparams(("arbitrary",)),
        name="rwkv_scan",
    )(r, lw, k, v, a, b, g, par)


def _outproj_kernel(x_ref, a_ref, y_ref, wa_ref, wy_ref, o_ref):
    o_ref[...] = (x_ref[...]
                  + jnp.dot(a_ref[...], wa_ref[...], preferred_element_type=F32)
                  + jnp.dot(y_ref[...], wy_ref[...], preferred_element_type=F32))


def _outproj(x, a_out, y, w_out, tm=1024, tn=512):
    T, D = x.shape
    tm = min(tm, T)
    return pl.pallas_call(
        _outproj_kernel,
        grid=(T // tm, D // tn),
        in_specs=[pl.BlockSpec((tm, tn), lambda i, j: (i, j)),
                  pl.BlockSpec((tm, ATT_W), lambda i, j: (i, 0)),
                  pl.BlockSpec((tm, RW_W), lambda i, j: (i, 0)),
                  pl.BlockSpec((ATT_W, tn), lambda i, j: (0, j)),
                  pl.BlockSpec((RW_W, tn), lambda i, j: (1, j))],
        out_specs=pl.BlockSpec((tm, tn), lambda i, j: (i, j)),
        out_shape=jax.ShapeDtypeStruct((T, D), F32),
        compiler_params=_cparams(("parallel", "arbitrary")),
        name="out_proj",
    )(x, a_out, y, w_out, w_out)


def _mlp_kernel(x_ref, g_ref, up_ref, dn_ref, o_ref, h_ref):
    f = pl.program_id(1)

    @pl.when(f == 0)
    def _():
        x = x_ref[...]
        ms = jnp.mean(x * x, axis=-1, keepdims=True)
        h_ref[...] = (x * lax.rsqrt(ms + RMS_EPS) * g_ref[...]).astype(BF16)
        o_ref[...] = x

    u = jnp.maximum(jnp.dot(h_ref[...], up_ref[...], preferred_element_type=F32), 0.0)
    o_ref[...] += jnp.dot((u * u).astype(BF16), dn_ref[...], preferred_element_type=F32)


def _mlp(x, g, up, down, tm=512, tf=512):
    T, D = x.shape
    F = up.shape[1]
    tm = min(tm, T)
    return pl.pallas_call(
        _mlp_kernel,
        grid=(T // tm, F // tf),
        in_specs=[pl.BlockSpec((tm, D), lambda i, f: (i, 0)),
                  pl.BlockSpec((1, D), lambda i, f: (0, 0)),
                  pl.BlockSpec((D, tf), lambda i, f: (0, f)),
                  pl.BlockSpec((tf, D), lambda i, f: (f, 0))],
        out_specs=pl.BlockSpec((tm, D), lambda i, f: (i, 0)),
        out_shape=jax.ShapeDtypeStruct((T, D), F32),
        scratch_shapes=[pltpu.VMEM((tm, D), BF16)],
        compiler_params=_cparams(("parallel", "arbitrary")),
        name="mlp",
    )(x, g.reshape(1, D), up, down)


def kernel(x, norm_mix, norm_mlp, w_in, w_out, qk_norm_q, qk_norm_k, diff_lambda_q, diff_lambda_k,
           diff_subln, rwkv_mu, rwkv_w0, rwkv_w2, rwkv_a0, rwkv_a2, rwkv_g2, rwkv_k_k, rwkv_k_a,
           rwkv_r_k, rwkv_ln_w, rwkv_ln_b, rwkv_v0, rwkv_v1, rwkv_v2, mlp_up, mlp_down):
    B, T, D = x.shape
    assert B == 1 and D == D_MODEL
    xs = x.reshape(T, D)
    n_qk = 2 * ATT_W
    slopes = jnp.asarray([2.0 ** (-8.0 * (i + 1) / N_HEADS_A) for i in range(N_HEADS_A)], F32)
    v_first = None
    for l in range(DEPTH):
        w = w_in[l]
        w_qk = w[:, :n_qk].astype(BF16)
        w_v = w[:, n_qk:n_qk + ATT_W].astype(BF16)
        w_rkv = w[:, n_qk + ATT_W:n_qk + ATT_W + 3 * RW_W].astype(BF16)
        w_lat = jnp.pad(w[:, n_qk + ATT_W + 3 * RW_W:], ((0, 0), (0, LAT_PAD - LAT_W))).astype(BF16)

        gq = jnp.tile(qk_norm_q[l].reshape(-1) * (LOG2E / math.sqrt(MAP_DIM)), N_HEADS_A)
        gk = jnp.tile(qk_norm_k[l].reshape(-1), N_HEADS_A)
        qk = _proj(xs, norm_mix[l], w_qk, BF16, gain=jnp.concatenate([gq, gk]))
        v_d = _proj(xs, norm_mix[l], w_v, BF16)
        rkv = _proj(xs, norm_mix[l], w_rkv, F32)
        lat = _proj(xs, norm_mix[l], w_lat, F32)

        lam_init = 0.8 - 0.6 * math.exp(-0.3 * l)
        lq, lk = diff_lambda_q[l], diff_lambda_k[l]
        lam = jnp.exp(jnp.sum(lq[0] * lk[0])) - jnp.exp(jnp.sum(lq[1] * lk[1])) + lam_init
        hp = jnp.zeros((N_HEADS_A, 8, HEAD_A), F32)
        hp = hp.at[:, 0, :].set((slopes * LOG2E)[:, None])
        hp = hp.at[:, 1, :].set(lam)
        hp = hp.at[:, 2, :].set(diff_subln[l][None, :] * (1.0 - lam_init))
        a_out = _attention(qk, v_d, hp)

        mu = rwkv_mu[l]
        mu_rkv = mu[:3 * RW_W].reshape(1, -1)
        mu_lat = jnp.pad(mu[3 * RW_W:], (0, LAT_PAD - LAT_W)).reshape(1, -1)
        has_gate = l > 0
        v0 = rwkv_v0[l - 1] if has_gate else jnp.zeros((RW_W,), F32)
        v1 = rwkv_v1[l - 1] if has_gate else jnp.zeros((RW_W, 32), F32)
        v2 = rwkv_v2[l - 1] if has_gate else jnp.zeros((32, RW_W), F32)
        vecs = jnp.zeros((8, RW_W), F32)
        for idx, vec in enumerate((rwkv_w0[l], rwkv_a0[l], rwkv_k_k[l], rwkv_k_a[l], v0)):
            vecs = vecs.at[idx].set(vec)
        vf = v_first if has_gate else rkv
        r, lw, k, v, a, b, g = _rwkv_prep(rkv, lat, vf, mu_rkv, mu_lat, vecs, rwkv_w2[l], rwkv_a2[l],
                                          rwkv_g2[l], v1, v2, has_gate)
        if l == 0:
            v_first = v
        par = jnp.zeros((8, RW_W), F32)
        par = par.at[0].set(rwkv_ln_w[l]).at[1].set(rwkv_ln_b[l]).at[2].set(rwkv_r_k[l].reshape(-1))
        y = _rwkv_scan(r, lw, k, v, a, b, g, par)

        xs = _outproj(xs, a_out, y, w_out[l].astype(BF16))
        xs = _mlp(xs, norm_mlp[l], mlp_up[l].astype(BF16), mlp_down[l].astype(BF16))
    return xs.reshape(B, T, D)
```

```python
import functools
import math

import jax
import jax.numpy as jnp
from jax import lax
from jax.experimental import pallas as pl
from jax.experimental.pallas import tpu as pltpu

F32 = jnp.float32
BF16 = jnp.bfloat16

D_MODEL = 2048
DEPTH = 4
CHUNK = 64
N_HEADS_A = 8
HEAD_A = 128
MAP_DIM = 64
ATT_W = N_HEADS_A * HEAD_A
RW_W = 1024
RW_HEAD = 64
DECAY_LORA = 64
AAA_LORA = 64
GATE_LORA = 160
LAT_W = DECAY_LORA + AAA_LORA + GATE_LORA
LAT_PAD = 384
D_FF = 4 * D_MODEL
RMS_EPS = 1e-6
RWKV_GN_EPS = 64e-5
NEG_BIG = -1e30
LOG2E = 1.4426950408889634

LANES = 128
QUAD = 4 * RW_HEAD
RCHUNK = 64
SUB = 16
VMEM_LIMIT = 56 * 1024 * 1024

HI = lax.Precision.HIGHEST


def _cparams(sem):
    return pltpu.CompilerParams(dimension_semantics=sem, vmem_limit_bytes=VMEM_LIMIT)


def _dot(a, b):
    return jnp.dot(a.astype(BF16), b.astype(BF16), preferred_element_type=F32)


def _dot_nt(a, b):
    return lax.dot_general(a.astype(BF16), b.astype(BF16), (((1,), (1,)), ((), ())),
                           preferred_element_type=F32)


def _dot_tn(a, b):
    return lax.dot_general(a.astype(BF16), b.astype(BF16), (((0,), (0,)), ((), ())),
                           preferred_element_type=F32)


def _dot_hi(a, b):
    return jnp.dot(a, b, precision=HI, preferred_element_type=F32)


def _group_ones(n, group):
    r = lax.broadcasted_iota(jnp.int32, (n, n), 0) // group
    c = lax.broadcasted_iota(jnp.int32, (n, n), 1) // group
    return (r == c).astype(F32)


def _proj_kernel(x_ref, g_ref, w_ref, *rest, qk):
    if qk:
        gain_ref, o_ref, h_ref = rest
    else:
        o_ref, h_ref = rest

    @pl.when(pl.program_id(1) == 0)
    def _():
        x = x_ref[...]
        ms = jnp.mean(x * x, axis=-1, keepdims=True)
        h_ref[...] = (x * lax.rsqrt(ms + RMS_EPS) * g_ref[...]).astype(BF16)

    acc = jnp.dot(h_ref[...], w_ref[...], preferred_element_type=F32)
    if not qk:
        o_ref[...] = acc.astype(o_ref.dtype)
        return
    tn = acc.shape[1]
    ones = _group_ones(LANES, MAP_DIM).astype(BF16)
    lane = lax.broadcasted_iota(jnp.int32, (1, LANES), 1)
    first = lane < MAP_DIM
    for s in range(tn // LANES):
        y = acc[:, s * LANES:(s + 1) * LANES]
        ss = jnp.dot((y * y).astype(BF16), ones, preferred_element_type=F32)
        yn = y * lax.rsqrt(ss * (1.0 / MAP_DIM) + RMS_EPS) * gain_ref[:, s * LANES:(s + 1) * LANES]
        o_ref[0, :, s * LANES:(s + 1) * LANES] = jnp.where(first, yn, 0.0).astype(o_ref.dtype)
        o_ref[1, :, s * LANES:(s + 1) * LANES] = jnp.where(
            first, pltpu.roll(yn, MAP_DIM, axis=1), 0.0).astype(o_ref.dtype)


def _proj(x, g, w, out_dtype, gain=None, tm=1024, tn=512):
    T, D = x.shape
    N = w.shape[1]
    tm = min(tm, T)
    tn = min(tn, N)
    if N % tn:
        tn = N
    qk = gain is not None
    in_specs = [pl.BlockSpec((tm, D), lambda i, j: (i, 0)),
                pl.BlockSpec((1, D), lambda i, j: (0, 0)),
                pl.BlockSpec((D, tn), lambda i, j: (0, j))]
    args = [x, g.reshape(1, D), w]
    if qk:
        in_specs.append(pl.BlockSpec((1, tn), lambda i, j: (0, j)))
        args.append(gain.reshape(1, N))
        out_shape = jax.ShapeDtypeStruct((2, T, N), out_dtype)
        out_spec = pl.BlockSpec((2, tm, tn), lambda i, j: (0, i, j))
    else:
        out_shape = jax.ShapeDtypeStruct((T, N), out_dtype)
        out_spec = pl.BlockSpec((tm, tn), lambda i, j: (i, j))
    return pl.pallas_call(
        functools.partial(_proj_kernel, qk=qk),
        grid=(T // tm, N // tn),
        in_specs=in_specs,
        out_specs=out_spec,
        out_shape=out_shape,
        scratch_shapes=[pltpu.VMEM((tm, D), BF16)],
        compiler_params=_cparams(("parallel", "arbitrary")),
        name="proj_qk" if qk else "proj",
    )(*args)


def _attn_kernel(q_ref, k_ref, v_ref, hp_ref, o_ref, s_ref, p_ref, al_ref, m_ref, acc_ref, *, bq):
    i = pl.program_id(1)
    c = hp_ref[0, 0:1, 0:1]
    lam = hp_ref[0, 1:2, :]
    sub = hp_ref[0, 2:3, :]
    neg_inf = float("-inf")

    s_ref[1] = jnp.full(s_ref.shape[1:], neg_inf, F32)
    p_ref[0] = jnp.zeros(p_ref.shape[1:], BF16)
    al_ref[0] = jnp.ones(al_ref.shape[1:], F32)
    m_ref[...] = jnp.full(m_ref.shape, NEG_BIG, F32)
    acc_ref[...] = jnp.zeros(acc_ref.shape, F32)

    kj = lax.broadcasted_iota(jnp.int32, (1, bq), 1).astype(F32)
    ckj = c * kj
    ones_col = (lax.broadcasted_iota(jnp.int32, (bq, HEAD_A), 1) == 0).astype(BF16)

    def v_ext(tile):
        start = pl.multiple_of(jnp.maximum(tile, 0) * bq, bq)
        return jnp.concatenate([v_ref[pl.ds(start, bq), :], ones_col], axis=1)

    def softmax_step(mp, s):
        m_old = m_ref[mp]
        m_new = jnp.maximum(m_old, jnp.max(s, axis=-1, keepdims=True))
        m_ref[mp] = m_new
        return jnp.exp2(m_old - m_new), jnp.exp2(s - m_new).astype(BF16)

    def stage_c(slot, tile, mp):
        acc_ref[mp] = al_ref[slot, mp] * acc_ref[mp] + jnp.dot(
            p_ref[slot, mp], v_ext(tile), preferred_element_type=F32)

    def stage_b(slot, mp):
        alpha, p = softmax_step(mp, s_ref[slot, mp])
        al_ref[slot, mp] = alpha
        p_ref[slot, mp] = p

    def stage_a(slot, j, mp):
        start = pl.multiple_of(j * bq, bq)
        bias = ckj - c * ((i - j) * bq).astype(F32)
        s_ref[slot, mp] = _dot_nt(q_ref[mp], k_ref[mp, pl.ds(start, bq), :]) + bias

    def body(t, carry):
        slot = t % 2
        for mp in range(2):
            stage_c(slot, t - 2, mp)
        for mp in range(2):
            stage_b(1 - slot, mp)
        for mp in range(2):
            stage_a(slot, t, mp)
        return carry

    lax.fori_loop(0, i, body, 0)

    slot = i % 2
    for mp in range(2):
        stage_c(slot, i - 2, mp)
        stage_b(1 - slot, mp)
        stage_c(1 - slot, i - 1, mp)
    start = pl.multiple_of(i * bq, bq)
    vx = v_ext(i)
    qi = lax.broadcasted_iota(jnp.int32, (bq, bq), 0)
    kk = lax.broadcasted_iota(jnp.int32, (bq, bq), 1)
    allowed = (kk // CHUNK) <= (qi // CHUNK)
    dbias = jnp.where(allowed, c * jnp.minimum(kk, 2 * qi - kk).astype(F32), neg_inf)
    for mp in range(2):
        s = _dot_nt(q_ref[mp], k_ref[mp, pl.ds(start, bq), :]) + dbias
        alpha, p = softmax_step(mp, s)
        acc_ref[mp] = alpha * acc_ref[mp] + jnp.dot(p, vx, preferred_element_type=F32)

    a1 = acc_ref[0]
    a2 = acc_ref[1]
    o = (a1[:, :HEAD_A] / a1[:, HEAD_A:HEAD_A + 1]
         - lam * (a2[:, :HEAD_A] / a2[:, HEAD_A:HEAD_A + 1]))
    ms = jnp.mean(o * o, axis=-1, keepdims=True)
    o_ref[...] = (o * lax.rsqrt(ms + RMS_EPS) * sub).astype(o_ref.dtype)


ATT_BQ = 512


def _attention(qk, v, hp):
    T = v.shape[0]
    bq = min(ATT_BQ, T)
    return pl.pallas_call(
        functools.partial(_attn_kernel, bq=bq),
        grid=(N_HEADS_A, T // bq),
        in_specs=[pl.BlockSpec((2, bq, HEAD_A), lambda h, i: (0, i, h)),
                  pl.BlockSpec((2, T, HEAD_A), lambda h, i: (0, 0, N_HEADS_A + h)),
                  pl.BlockSpec((T, HEAD_A), lambda h, i: (0, h)),
                  pl.BlockSpec((1, 8, HEAD_A), lambda h, i: (h, 0, 0))],
        out_specs=pl.BlockSpec((bq, HEAD_A), lambda h, i: (i, h)),
        out_shape=jax.ShapeDtypeStruct((T, ATT_W), BF16),
        scratch_shapes=[pltpu.VMEM((2, 2, bq, bq), F32),
                        pltpu.VMEM((2, 2, bq, bq), BF16),
                        pltpu.VMEM((2, 2, bq, 1), F32),
                        pltpu.VMEM((2, bq, 1), F32),
                        pltpu.VMEM((2, bq, 2 * HEAD_A), F32)],
        compiler_params=_cparams(("parallel", "arbitrary")),
        name="diff_attn",
    )(qk, qk, v, hp)


def _shift(p, carry_row, mu):
    rolled = pltpu.roll(p, 1, axis=0)
    row = lax.broadcasted_iota(jnp.int32, p.shape, 0)
    prev = jnp.where(row == 0, carry_row, rolled)
    return p + (prev - p) * mu


def _prep_kernel(rkv_ref, lat_ref, vf_ref, mu_ref, mul_ref, vec_ref, w2_ref, a2_ref, g2_ref,
                 v1_ref, v2_ref,
                 r_ref, lw_ref, k_ref, v_ref, a_ref, b_ref, g_ref,
                 c_rkv, c_lat, *, has_gate):
    @pl.when(pl.program_id(0) == 0)
    def _():
        c_rkv[...] = jnp.zeros(c_rkv.shape, F32)
        c_lat[...] = jnp.zeros(c_lat.shape, F32)

    p = rkv_ref[...]
    lt = lat_ref[...]
    n = p.shape[0]
    ps = _shift(p, c_rkv[0:1, :], mu_ref[...])
    ls = _shift(lt, c_lat[0:1, :], mul_ref[...])
    c_rkv[0:1, :] = p[n - 1:n, :]
    c_lat[0:1, :] = lt[n - 1:n, :]

    w0, a0, k_k, k_a, v0 = (vec_ref[i:i + 1, :] for i in range(5))
    r = ps[:, 0:RW_W]
    kr = ps[:, RW_W:2 * RW_W]
    vr = ps[:, 2 * RW_W:3 * RW_W]
    lat_w = ls[:, 0:DECAY_LORA]
    lat_a = ls[:, DECAY_LORA:DECAY_LORA + AAA_LORA]
    lat_g = ls[:, DECAY_LORA + AAA_LORA:LAT_W]

    xw = w0 + _dot_hi(jnp.tanh(lat_w), w2_ref[...])
    w_log = -jax.nn.softplus(-xw) - 0.5
    lw_ref[...] = -jnp.exp(w_log)
    if has_gate:
        gate = jax.nn.sigmoid(v0 + _dot_hi(_dot_hi(vr, v1_ref[...]), v2_ref[...]))
        vr = vr + (vf_ref[...] - vr) * gate
    a_rate = jax.nn.sigmoid(a0 + _dot_hi(lat_a, a2_ref[...]))
    g_ref[...] = _dot_hi(jax.nn.sigmoid(lat_g), g2_ref[...])
    kk = kr * k_k
    ones = _group_ones(LANES, RW_HEAD)
    for s in range(RW_W // LANES):
        sl = slice(s * LANES, (s + 1) * LANES)
        ks = kk[:, sl]
        nrm = jnp.sqrt(_dot_hi(ks * ks, ones))
        kn = ks / jnp.maximum(nrm, 1e-12)
        a_ref[:, sl] = -kn
        b_ref[:, sl] = kn * a_rate[:, sl]
    r_ref[...] = r
    k_ref[...] = kr * (1.0 + (a_rate - 1.0) * k_a)
    v_ref[...] = vr


def _rwkv_prep(rkv, lat, v_first, mu_rkv, mu_lat, vecs, w2, a2, g2, v1, v2, has_gate, tp=256):
    T = rkv.shape[0]
    tp = min(tp, T)
    row = lambda w: pl.BlockSpec((tp, w), lambda i: (i, 0))
    full = lambda a: pl.BlockSpec(a.shape, lambda i: (0,) * a.ndim)
    outs = [jax.ShapeDtypeStruct((T, RW_W), F32)] * 7
    return pl.pallas_call(
        functools.partial(_prep_kernel, has_gate=has_gate),
        grid=(T // tp,),
        in_specs=[row(3 * RW_W), row(LAT_PAD), row(RW_W), full(mu_rkv), full(mu_lat), full(vecs),
                  full(w2), full(a2), full(g2), full(v1), full(v2)],
        out_specs=[row(RW_W)] * 7,
        out_shape=outs,
        scratch_shapes=[pltpu.VMEM((8, 3 * RW_W), F32), pltpu.VMEM((8, LAT_PAD), F32)],
        compiler_params=_cparams(("arbitrary",)),
        name="rwkv_prep",
    )(rkv, lat, v_first, mu_rkv, mu_lat, vecs, w2, a2, g2, v1, v2)


def _bdiag(y, masks):
    yb = y.astype(BF16)
    return jnp.concatenate([yb * m for m in masks], axis=0)


def _split(x, parts):
    pieces = []
    for _ in range(parts):
        h = x.astype(BF16)
        pieces.append(h)
        x = x - h.astype(F32)
    return pieces


def _dot_rhs_split(a, x, parts):
    n = x.shape[1]
    out = jnp.dot(a, jnp.concatenate(_split(x, parts), axis=1), preferred_element_type=F32)
    return sum(out[:, i * n:(i + 1) * n] for i in range(parts))


def _dot_lhs_split(x, a, parts):
    m = x.shape[0]
    out = jnp.dot(jnp.concatenate(_split(x, parts), axis=0), a, preferred_element_type=F32)
    return sum(out[i * m:(i + 1) * m, :] for i in range(parts))


def _scan_kernel(r_ref, lw_ref, k_ref, v_ref, a_ref, b_ref, g_ref, par_ref, o_ref, s_ref, *, n_chunks):
    C = RCHUNK
    n_quads = RW_W // QUAD

    @pl.when(pl.program_id(0) == 0)
    def _():
        s_ref[...] = jnp.zeros(s_ref.shape, F32)

    lane = lax.broadcasted_iota(jnp.int32, (1, QUAD), 1) // RW_HEAD
    hm1 = [(lane == q).astype(BF16) for q in range(4)]
    hm2 = [jnp.concatenate([m, m], axis=1) for m in hm1]
    hm3 = [jnp.concatenate([m, m, m], axis=1) for m in hm1]
    t_i = lax.broadcasted_iota(jnp.int32, (C, QUAD), 0)
    s_i = lax.broadcasted_iota(jnp.int32, (C, QUAD), 1) % C
    strict = (s_i < t_i).astype(F32)
    incl = (s_i <= t_i).astype(F32)
    same_sub = ((s_i // SUB) == (t_i // SUB)).astype(F32)
    eye_q = (s_i == t_i).astype(F32)
    bd = _group_ones(QUAD, RW_HEAD)
    eye = (lax.broadcasted_iota(jnp.int32, (QUAD, QUAD), 0)
           == lax.broadcasted_iota(jnp.int32, (QUAD, QUAD), 1)).astype(F32)
    tri = (lax.broadcasted_iota(jnp.int32, (C, C), 1)
           <= lax.broadcasted_iota(jnp.int32, (C, C), 0)).astype(BF16)
    bd16 = bd.astype(BF16)
    inv_n = 1.0 / RW_HEAD

    def one_quad(rs, qd):
        cs = slice(qd * QUAD, (qd + 1) * QUAD)
        ln_w = par_ref[0:1, cs]
        ln_b = par_ref[1:2, cs]
        r_k = par_ref[2:3, cs]
        r = r_ref[rs, cs]
        lw = lw_ref[rs, cs]
        k = k_ref[rs, cs]
        v = v_ref[rs, cs]
        S = s_ref[qd]

        cw = _dot_rhs_split(tri, lw, 3)
        yield
        e_pos = jnp.exp(cw)
        e_neg = jnp.exp(-cw)
        at = a_ref[rs, cs] * jnp.exp(cw - lw)
        rt = r * e_pos
        bt = b_ref[rs, cs] * e_neg
        kt = k * e_neg
        wc = e_pos[C - 1:C, :]

        z = jnp.concatenate([_bdiag(bt, hm1), _bdiag(kt, hm1)], axis=0)
        g4 = _dot_nt(jnp.concatenate([at, rt], axis=0), z)
        yield
        a_ab = g4[0:C, 0:QUAD] * strict
        a_ak = g4[0:C, QUAD:2 * QUAD] * strict
        a_rb = g4[C:2 * C, 0:QUAD] * incl
        a_rk = g4[C:2 * C, QUAD:2 * QUAD] * incl

        d1 = a_ab * same_sub
        a_off = a_ab - d1
        v_bd = _bdiag(v, hm1)
        akv = _dot(a_ak, v_bd)
        d2 = _dot(d1, _bdiag(d1, hm1))
        yield
        d2_bd = _bdiag(d2, hm1)
        d4 = _dot(d2, d2_bd)
        p = eye_q + d1
        p = p + _dot(p, d2_bd)
        yield
        d4_bd = _bdiag(d4, hm1)
        d8 = _dot(d4, d4_bd)
        p = p + _dot(p, d4_bd)
        yield
        t_d = p + _dot(p, _bdiag(d8, hm1))
        yield
        nx = _dot(t_d, _bdiag(jnp.concatenate([a_off, at, akv], axis=1), hm3))
        yield
        n_m = nx[:, 0:QUAD].astype(BF16)
        cc = nx[:, QUAD:3 * QUAD]
        u = cc
        for _ in range(C // SUB - 1):
            u = cc + jnp.dot(n_m, _bdiag(u, hm2), preferred_element_type=F32)
            yield
        a_hat = u[:, 0:QUAD]
        u_hat = u[:, QUAD:2 * QUAD]

        m_c = (eye + _dot_tn(a_hat, bt) * bd) * wc
        n_c = (_dot_tn(u_hat, bt) + _dot_tn(v, kt)) * bd * wc
        q = rt + _dot(a_rb, _bdiag(a_hat, hm1))
        y0 = _dot(jnp.concatenate([a_rb, a_rk], axis=1),
                  jnp.concatenate([_bdiag(u_hat, hm1), v_bd], axis=0))
        yield
        y = _dot_nt(q, S) + y0
        s_ref[qd] = _dot(S, m_c) + n_c
        yield

        mean = _dot_lhs_split(y, bd16, 2) * inv_n
        bonus = _dot_lhs_split(r * k * r_k, bd16, 2) * v
        yield
        dlt = y - mean
        var = _dot_lhs_split(dlt * dlt, bd16, 2) * inv_n
        yield
        yn = dlt * lax.rsqrt(var + RWKV_GN_EPS) * ln_w + ln_b
        o_ref[rs, cs] = ((yn + bonus) * g_ref[rs, cs]).astype(o_ref.dtype)

    def chunk_body(ci, carry):
        rs = pl.ds(pl.multiple_of(ci * C, C), C)
        pending = [one_quad(rs, qd) for qd in range(n_quads)]
        while pending:
            pending = [g for g in pending if next(g, True) is None]
        return carry

    lax.fori_loop(0, n_chunks, chunk_body, 0)


def _rwkv_scan(r, lw, k, v, a, b, g, par, ls=256):
    T = r.shape[0]
    ls = min(ls, T)
    row = pl.BlockSpec((ls, RW_W), lambda i: (i, 0))
    return pl.pallas_call(
        functools.partial(_scan_kernel, n_chunks=ls // RCHUNK),
        grid=(T // ls,),
        in_specs=[row] * 7 + [pl.BlockSpec(par.shape, lambda i: (0, 0))],
        out_specs=row,
        out_shape=jax.ShapeDtypeStruct((T, RW_W), BF16),
        scratch_shapes=[pltpu.VMEM((RW_W // QUAD, QUAD, QUAD), F32)],
        compiler_params=_cparams(("arbitrary",)),
        name="rwkv_scan",
    )(r, lw, k, v, a, b, g, par)


def _outproj_kernel(x_ref, a_ref, y_ref, wa_ref, wy_ref, o_ref):
    o_ref[...] = (x_ref[...]
                  + jnp.dot(a_ref[...], wa_ref[...], preferred_element_type=F32)
                  + jnp.dot(y_ref[...], wy_ref[...], preferred_element_type=F32))


def _outproj(x, a_out, y, w_out, tm=1024, tn=512):
    T, D = x.shape
    tm = min(tm, T)
    return pl.pallas_call(
        _outproj_kernel,
        grid=(T // tm, D // tn),
        in_specs=[pl.BlockSpec((tm, tn), lambda i, j: (i, j)),
                  pl.BlockSpec((tm, ATT_W), lambda i, j: (i, 0)),
                  pl.BlockSpec((tm, RW_W), lambda i, j: (i, 0)),
                  pl.BlockSpec((ATT_W, tn), lambda i, j: (0, j)),
                  pl.BlockSpec((RW_W, tn), lambda i, j: (1, j))],
        out_specs=pl.BlockSpec((tm, tn), lambda i, j: (i, j)),
        out_shape=jax.ShapeDtypeStruct((T, D), F32),
        compiler_params=_cparams(("parallel", "arbitrary")),
        name="out_proj",
    )(x, a_out, y, w_out, w_out)


def _mlp_kernel(x_ref, g_ref, up_ref, dn_ref, o_ref, h_ref):
    f = pl.program_id(1)

    @pl.when(f == 0)
    def _():
        x = x_ref[...]
        ms = jnp.mean(x * x, axis=-1, keepdims=True)
        h_ref[...] = (x * lax.rsqrt(ms + RMS_EPS) * g_ref[...]).astype(BF16)
        o_ref[...] = x

    u = jnp.maximum(jnp.dot(h_ref[...], up_ref[...], preferred_element_type=F32), 0.0)
    o_ref[...] += jnp.dot((u * u).astype(BF16), dn_ref[...], preferred_element_type=F32)


def _mlp(x, g, up, down, tm=512, tf=512):
    T, D = x.shape
    F = up.shape[1]
    tm = min(tm, T)
    return pl.pallas_call(
        _mlp_kernel,
        grid=(T // tm, F // tf),
        in_specs=[pl.BlockSpec((tm, D), lambda i, f: (i, 0)),
                  pl.BlockSpec((1, D), lambda i, f: (0, 0)),
                  pl.BlockSpec((D, tf), lambda i, f: (0, f)),
                  pl.BlockSpec((tf, D), lambda i, f: (f, 0))],
        out_specs=pl.BlockSpec((tm, D), lambda i, f: (i, 0)),
        out_shape=jax.ShapeDtypeStruct((T, D), F32),
        scratch_shapes=[pltpu.VMEM((tm, D), BF16)],
        compiler_params=_cparams(("parallel", "arbitrary")),
        name="mlp",
    )(x, g.reshape(1, D), up, down)


def kernel(x, norm_mix, norm_mlp, w_in, w_out, qk_norm_q, qk_norm_k, diff_lambda_q, diff_lambda_k,
           diff_subln, rwkv_mu, rwkv_w0, rwkv_w2, rwkv_a0, rwkv_a2, rwkv_g2, rwkv_k_k, rwkv_k_a,
           rwkv_r_k, rwkv_ln_w, rwkv_ln_b, rwkv_v0, rwkv_v1, rwkv_v2, mlp_up, mlp_down):
    B, T, D = x.shape
    assert B == 1 and D == D_MODEL
    xs = x.reshape(T, D)
    n_qk = 2 * ATT_W
    slopes = jnp.asarray([2.0 ** (-8.0 * (i + 1) / N_HEADS_A) for i in range(N_HEADS_A)], F32)
    v_first = None
    for l in range(DEPTH):
        w = w_in[l]
        w_qk = w[:, :n_qk].astype(BF16)
        w_v = w[:, n_qk:n_qk + ATT_W].astype(BF16)
        w_rkv = w[:, n_qk + ATT_W:n_qk + ATT_W + 3 * RW_W].astype(BF16)
        w_lat = jnp.pad(w[:, n_qk + ATT_W + 3 * RW_W:], ((0, 0), (0, LAT_PAD - LAT_W))).astype(BF16)

        gq = jnp.tile(qk_norm_q[l].reshape(-1) * (LOG2E / math.sqrt(MAP_DIM)), N_HEADS_A)
        gk = jnp.tile(qk_norm_k[l].reshape(-1), N_HEADS_A)
        qk = _proj(xs, norm_mix[l], w_qk, BF16, gain=jnp.concatenate([gq, gk]))
        v_d = _proj(xs, norm_mix[l], w_v, BF16)
        rkv = _proj(xs, norm_mix[l], w_rkv, F32)
        lat = _proj(xs, norm_mix[l], w_lat, F32)

        lam_init = 0.8 - 0.6 * math.exp(-0.3 * l)
        lq, lk = diff_lambda_q[l], diff_lambda_k[l]
        lam = jnp.exp(jnp.sum(lq[0] * lk[0])) - jnp.exp(jnp.sum(lq[1] * lk[1])) + lam_init
        hp = jnp.zeros((N_HEADS_A, 8, HEAD_A), F32)
        hp = hp.at[:, 0, :].set((slopes * LOG2E)[:, None])
        hp = hp.at[:, 1, :].set(lam)
        hp = hp.at[:, 2, :].set(diff_subln[l][None, :] * (1.0 - lam_init))
        a_out = _attention(qk, v_d, hp)

        mu = rwkv_mu[l]
        mu_rkv = mu[:3 * RW_W].reshape(1, -1)
        mu_lat = jnp.pad(mu[3 * RW_W:], (0, LAT_PAD - LAT_W)).reshape(1, -1)
        has_gate = l > 0
        v0 = rwkv_v0[l - 1] if has_gate else jnp.zeros((RW_W,), F32)
        v1 = rwkv_v1[l - 1] if has_gate else jnp.zeros((RW_W, 32), F32)
        v2 = rwkv_v2[l - 1] if has_gate else jnp.zeros((32, RW_W), F32)
        vecs = jnp.zeros((8, RW_W), F32)
        for idx, vec in enumerate((rwkv_w0[l], rwkv_a0[l], rwkv_k_k[l], rwkv_k_a[l], v0)):
            vecs = vecs.at[idx].set(vec)
        vf = v_first if has_gate else rkv
        r, lw, k, v, a, b, g = _rwkv_prep(rkv, lat, vf, mu_rkv, mu_lat, vecs, rwkv_w2[l], rwkv_a2[l],
                                          rwkv_g2[l], v1, v2, has_gate)
        if l == 0:
            v_first = v
        par = jnp.zeros((8, RW_W), F32)
        par = par.at[0].set(rwkv_ln_w[l]).at[1].set(rwkv_ln_b[l]).at[2].set(rwkv_r_k[l].reshape(-1))
        y = _rwkv_scan(r, lw, k, v, a, b, g, par)

        xs = _outproj(xs, a_out, y, w_out[l].astype(BF16))
        xs = _mlp(xs, norm_mlp[l], mlp_up[l].astype(BF16), mlp_down[l].astype(BF16))
    return xs.reshape(B, T, D)
```

```python
import functools
import math

import jax
import jax.numpy as jnp
from jax import lax
from jax.experimental import pallas as pl
from jax.experimental.pallas import tpu as pltpu

F32 = jnp.float32
BF16 = jnp.bfloat16

D_MODEL = 2048
DEPTH = 4
CHUNK = 64
N_HEADS_A = 8
HEAD_A = 128
MAP_DIM = 64
ATT_W = N_HEADS_A * HEAD_A
RW_W = 1024
RW_HEAD = 64
DECAY_LORA = 64
AAA_LORA = 64
GATE_LORA = 160
LAT_W = DECAY_LORA + AAA_LORA + GATE_LORA
LAT_PAD = 384
D_FF = 4 * D_MODEL
RMS_EPS = 1e-6
RWKV_GN_EPS = 64e-5
NEG_BIG = -1e30
LOG2E = 1.4426950408889634

LANES = 128
QUAD = 4 * RW_HEAD
RCHUNK = 64
SUB = 16
VMEM_LIMIT = 56 * 1024 * 1024

HI = lax.Precision.HIGHEST


def _cparams(sem):
    return pltpu.CompilerParams(dimension_semantics=sem, vmem_limit_bytes=VMEM_LIMIT)


def _dot(a, b):
    return jnp.dot(a.astype(BF16), b.astype(BF16), preferred_element_type=F32)


def _dot_nt(a, b):
    return lax.dot_general(a.astype(BF16), b.astype(BF16), (((1,), (1,)), ((), ())),
                           preferred_element_type=F32)


def _dot_tn(a, b):
    return lax.dot_general(a.astype(BF16), b.astype(BF16), (((0,), (0,)), ((), ())),
                           preferred_element_type=F32)


def _dot_hi(a, b):
    return jnp.dot(a, b, precision=HI, preferred_element_type=F32)


def _group_ones(n, group):
    r = lax.broadcasted_iota(jnp.int32, (n, n), 0) // group
    c = lax.broadcasted_iota(jnp.int32, (n, n), 1) // group
    return (r == c).astype(F32)


def _proj_kernel(x_ref, g_ref, w_ref, *rest, qk):
    if qk:
        gain_ref, o_ref, h_ref = rest
    else:
        o_ref, h_ref = rest

    @pl.when(pl.program_id(1) == 0)
    def _():
        x = x_ref[...]
        ms = jnp.mean(x * x, axis=-1, keepdims=True)
        h_ref[...] = (x * lax.rsqrt(ms + RMS_EPS) * g_ref[...]).astype(BF16)

    acc = jnp.dot(h_ref[...], w_ref[...], preferred_element_type=F32)
    if not qk:
        o_ref[...] = acc.astype(o_ref.dtype)
        return
    tn = acc.shape[1]
    ones = _group_ones(LANES, MAP_DIM).astype(BF16)
    lane = lax.broadcasted_iota(jnp.int32, (1, LANES), 1)
    first = lane < MAP_DIM
    for s in range(tn // LANES):
        y = acc[:, s * LANES:(s + 1) * LANES]
        ss = jnp.dot((y * y).astype(BF16), ones, preferred_element_type=F32)
        yn = y * lax.rsqrt(ss * (1.0 / MAP_DIM) + RMS_EPS) * gain_ref[:, s * LANES:(s + 1) * LANES]
        o_ref[0, :, s * LANES:(s + 1) * LANES] = jnp.where(first, yn, 0.0).astype(o_ref.dtype)
        o_ref[1, :, s * LANES:(s + 1) * LANES] = jnp.where(
            first, pltpu.roll(yn, MAP_DIM, axis=1), 0.0).astype(o_ref.dtype)


def _proj(x, g, w, out_dtype, gain=None, tm=1024, tn=512):
    T, D = x.shape
    N = w.shape[1]
    tm = min(tm, T)
    tn = min(tn, N)
    if N % tn:
        tn = N
    qk = gain is not None
    in_specs = [pl.BlockSpec((tm, D), lambda i, j: (i, 0)),
                pl.BlockSpec((1, D), lambda i, j: (0, 0)),
                pl.BlockSpec((D, tn), lambda i, j: (0, j))]
    args = [x, g.reshape(1, D), w]
    if qk:
        in_specs.append(pl.BlockSpec((1, tn), lambda i, j: (0, j)))
        args.append(gain.reshape(1, N))
        out_shape = jax.ShapeDtypeStruct((2, T, N), out_dtype)
        out_spec = pl.BlockSpec((2, tm, tn), lambda i, j: (0, i, j))
    else:
        out_shape = jax.ShapeDtypeStruct((T, N), out_dtype)
        out_spec = pl.BlockSpec((tm, tn), lambda i, j: (i, j))
    return pl.pallas_call(
        functools.partial(_proj_kernel, qk=qk),
        grid=(T // tm, N // tn),
        in_specs=in_specs,
        out_specs=out_spec,
        out_shape=out_shape,
        scratch_shapes=[pltpu.VMEM((tm, D), BF16)],
        compiler_params=_cparams(("parallel", "arbitrary")),
        name="proj_qk" if qk else "proj",
    )(*args)


def _attn_kernel(q_ref, k_ref, v_ref, hp_ref, o_ref, s_ref, pm_ref, p_ref, al_ref, m_ref, acc_ref, *, bq):
    i = pl.program_id(1)
    c = hp_ref[0, 0:1, 0:1]
    lam = hp_ref[0, 1:2, :]
    sub = hp_ref[0, 2:3, :]
    neg_inf = float("-inf")

    p_ref[1] = jnp.zeros(p_ref.shape[1:], BF16)
    al_ref[1] = jnp.ones(al_ref.shape[1:], F32)
    m_ref[...] = jnp.full(m_ref.shape, NEG_BIG, F32)
    acc_ref[...] = jnp.zeros(acc_ref.shape, F32)

    kj = lax.broadcasted_iota(jnp.int32, (1, bq), 1).astype(F32)
    ckj = c * kj
    ones_col = (lax.broadcasted_iota(jnp.int32, (bq, HEAD_A), 1) == 0).astype(BF16)

    def key_rows(step):
        return pl.ds(pl.multiple_of((i - jnp.maximum(step, 0)) * bq, bq), bq)

    def put_scores(slot, mp, s):
        s_ref[slot, mp] = s
        pm_ref[slot, mp] = functools.reduce(
            jnp.maximum, [s[:, n * LANES:(n + 1) * LANES] for n in range(bq // LANES)])

    def stage_c(step, mp):
        slot = step % 2
        vx = jnp.concatenate([v_ref[key_rows(step), :], ones_col], axis=1)
        acc_ref[mp] = al_ref[slot, mp] * acc_ref[mp] + jnp.dot(
            p_ref[slot, mp], vx, preferred_element_type=F32)

    def stage_b(step, mp):
        slot = step % 2
        m_old = m_ref[mp]
        m_new = jnp.maximum(m_old, jnp.max(pm_ref[slot, mp], axis=-1, keepdims=True))
        m_ref[mp] = m_new
        al_ref[slot, mp] = jnp.exp2(m_old - m_new)
        p_ref[slot, mp] = jnp.exp2(s_ref[slot, mp] - m_new).astype(BF16)

    def stage_a(step, mp):
        bias = ckj - c * (step * bq).astype(F32)
        put_scores(step % 2, mp, _dot_nt(q_ref[mp], k_ref[mp, key_rows(step), :]) + bias)

    qi = lax.broadcasted_iota(jnp.int32, (bq, bq), 0)
    kk = lax.broadcasted_iota(jnp.int32, (bq, bq), 1)
    allowed = (kk // CHUNK) <= (qi // CHUNK)
    dbias = jnp.where(allowed, c * jnp.minimum(kk, 2 * qi - kk).astype(F32), neg_inf)
    for mp in range(2):
        put_scores(0, mp, _dot_nt(q_ref[mp], k_ref[mp, key_rows(0), :]) + dbias)

    def body(t, carry):
        for mp in range(2):
            stage_c(t - 2, mp)
        for mp in range(2):
            stage_b(t - 1, mp)
        for mp in range(2):
            stage_a(t, mp)
        return carry

    n = i
    lax.fori_loop(1, n + 1, body, 0)

    for mp in range(2):
        stage_c(n - 1, mp)
        stage_b(n, mp)
        stage_c(n, mp)

    a1 = acc_ref[0]
    a2 = acc_ref[1]
    o = (a1[:, :HEAD_A] / a1[:, HEAD_A:HEAD_A + 1]
         - lam * (a2[:, :HEAD_A] / a2[:, HEAD_A:HEAD_A + 1]))
    ms = jnp.mean(o * o, axis=-1, keepdims=True)
    o_ref[...] = (o * lax.rsqrt(ms + RMS_EPS) * sub).astype(o_ref.dtype)


ATT_BQ = 512


def _attention(qk, v, hp):
    T = v.shape[0]
    bq = min(ATT_BQ, T)
    return pl.pallas_call(
        functools.partial(_attn_kernel, bq=bq),
        grid=(N_HEADS_A, T // bq),
        in_specs=[pl.BlockSpec((2, bq, HEAD_A), lambda h, i: (0, i, h)),
                  pl.BlockSpec((2, T, HEAD_A), lambda h, i: (0, 0, N_HEADS_A + h)),
                  pl.BlockSpec((T, HEAD_A), lambda h, i: (0, h)),
                  pl.BlockSpec((1, 8, HEAD_A), lambda h, i: (h, 0, 0))],
        out_specs=pl.BlockSpec((bq, HEAD_A), lambda h, i: (i, h)),
        out_shape=jax.ShapeDtypeStruct((T, ATT_W), BF16),
        scratch_shapes=[pltpu.VMEM((2, 2, bq, bq), F32),
                        pltpu.VMEM((2, 2, bq, LANES), F32),
                        pltpu.VMEM((2, 2, bq, bq), BF16),
                        pltpu.VMEM((2, 2, bq, 1), F32),
                        pltpu.VMEM((2, bq, 1), F32),
                        pltpu.VMEM((2, bq, 2 * HEAD_A), F32)],
        compiler_params=_cparams(("parallel", "arbitrary")),
        name="diff_attn",
    )(qk, qk, v, hp)


def _shift(p, carry_row, mu):
    rolled = pltpu.roll(p, 1, axis=0)
    row = lax.broadcasted_iota(jnp.int32, p.shape, 0)
    prev = jnp.where(row == 0, carry_row, rolled)
    return p + (prev - p) * mu


def _prep_kernel(rkv_ref, lat_ref, vf_ref, mu_ref, mul_ref, vec_ref, w2_ref, a2_ref, g2_ref,
                 v1_ref, v2_ref,
                 r_ref, lw_ref, k_ref, v_ref, a_ref, b_ref, g_ref,
                 c_rkv, c_lat, *, has_gate):
    @pl.when(pl.program_id(0) == 0)
    def _():
        c_rkv[...] = jnp.zeros(c_rkv.shape, F32)
        c_lat[...] = jnp.zeros(c_lat.shape, F32)

    p = rkv_ref[...]
    lt = lat_ref[...]
    n = p.shape[0]
    ps = _shift(p, c_rkv[0:1, :], mu_ref[...])
    ls = _shift(lt, c_lat[0:1, :], mul_ref[...])
    c_rkv[0:1, :] = p[n - 1:n, :]
    c_lat[0:1, :] = lt[n - 1:n, :]

    w0, a0, k_k, k_a, v0 = (vec_ref[i:i + 1, :] for i in range(5))
    r = ps[:, 0:RW_W]
    kr = ps[:, RW_W:2 * RW_W]
    vr = ps[:, 2 * RW_W:3 * RW_W]
    lat_w = ls[:, 0:DECAY_LORA]
    lat_a = ls[:, DECAY_LORA:DECAY_LORA + AAA_LORA]
    lat_g = ls[:, DECAY_LORA + AAA_LORA:LAT_W]

    xw = w0 + _dot_hi(jnp.tanh(lat_w), w2_ref[...])
    w_log = -jax.nn.softplus(-xw) - 0.5
    lw_ref[...] = -jnp.exp(w_log)
    if has_gate:
        gate = jax.nn.sigmoid(v0 + _dot_hi(_dot_hi(vr, v1_ref[...]), v2_ref[...]))
        vr = vr + (vf_ref[...] - vr) * gate
    a_rate = jax.nn.sigmoid(a0 + _dot_hi(lat_a, a2_ref[...]))
    g_ref[...] = _dot_hi(jax.nn.sigmoid(lat_g), g2_ref[...])
    kk = kr * k_k
    ones = _group_ones(LANES, RW_HEAD)
    for s in range(RW_W // LANES):
        sl = slice(s * LANES, (s + 1) * LANES)
        ks = kk[:, sl]
        nrm = jnp.sqrt(_dot_hi(ks * ks, ones))
        kn = ks / jnp.maximum(nrm, 1e-12)
        a_ref[:, sl] = -kn
        b_ref[:, sl] = kn * a_rate[:, sl]
    r_ref[...] = r
    k_ref[...] = kr * (1.0 + (a_rate - 1.0) * k_a)
    v_ref[...] = vr


def _rwkv_prep(rkv, lat, v_first, mu_rkv, mu_lat, vecs, w2, a2, g2, v1, v2, has_gate, tp=256):
    T = rkv.shape[0]
    tp = min(tp, T)
    row = lambda w: pl.BlockSpec((tp, w), lambda i: (i, 0))
    full = lambda a: pl.BlockSpec(a.shape, lambda i: (0,) * a.ndim)
    outs = [jax.ShapeDtypeStruct((T, RW_W), F32)] * 7
    return pl.pallas_call(
        functools.partial(_prep_kernel, has_gate=has_gate),
        grid=(T // tp,),
        in_specs=[row(3 * RW_W), row(LAT_PAD), row(RW_W), full(mu_rkv), full(mu_lat), full(vecs),
                  full(w2), full(a2), full(g2), full(v1), full(v2)],
        out_specs=[row(RW_W)] * 7,
        out_shape=outs,
        scratch_shapes=[pltpu.VMEM((8, 3 * RW_W), F32), pltpu.VMEM((8, LAT_PAD), F32)],
        compiler_params=_cparams(("arbitrary",)),
        name="rwkv_prep",
    )(rkv, lat, v_first, mu_rkv, mu_lat, vecs, w2, a2, g2, v1, v2)


def _bdiag(y, masks):
    yb = y.astype(BF16)
    return jnp.concatenate([yb * m for m in masks], axis=0)


def _split(x, parts):
    pieces = []
    for _ in range(parts):
        h = x.astype(BF16)
        pieces.append(h)
        x = x - h.astype(F32)
    return pieces


def _dot_rhs_split(a, x, parts):
    n = x.shape[1]
    out = jnp.dot(a, jnp.concatenate(_split(x, parts), axis=1), preferred_element_type=F32)
    return sum(out[:, i * n:(i + 1) * n] for i in range(parts))


def _dot_lhs_split(x, a, parts):
    m = x.shape[0]
    out = jnp.dot(jnp.concatenate(_split(x, parts), axis=0), a, preferred_element_type=F32)
    return sum(out[i * m:(i + 1) * m, :] for i in range(parts))


def _scan_kernel(r_ref, lw_ref, k_ref, v_ref, a_ref, b_ref, g_ref, par_ref, o_ref, s_ref, *, n_chunks):
    C = RCHUNK
    n_quads = RW_W // QUAD

    @pl.when(pl.program_id(0) == 0)
    def _():
        s_ref[...] = jnp.zeros(s_ref.shape, F32)

    lane = lax.broadcasted_iota(jnp.int32, (1, QUAD), 1) // RW_HEAD
    hm1 = [(lane == q).astype(BF16) for q in range(4)]
    hm2 = [jnp.concatenate([m, m], axis=1) for m in hm1]
    hm3 = [jnp.concatenate([m, m, m], axis=1) for m in hm1]
    t_i = lax.broadcasted_iota(jnp.int32, (C, QUAD), 0)
    s_i = lax.broadcasted_iota(jnp.int32, (C, QUAD), 1) % C
    strict = (s_i < t_i).astype(F32)
    incl = (s_i <= t_i).astype(F32)
    same_sub = ((s_i // SUB) == (t_i // SUB)).astype(F32)
    eye_q = (s_i == t_i).astype(F32)
    bd = _group_ones(QUAD, RW_HEAD)
    eye = (lax.broadcasted_iota(jnp.int32, (QUAD, QUAD), 0)
           == lax.broadcasted_iota(jnp.int32, (QUAD, QUAD), 1)).astype(F32)
    tri = (lax.broadcasted_iota(jnp.int32, (C, C), 1)
           <= lax.broadcasted_iota(jnp.int32, (C, C), 0)).astype(BF16)
    bd16 = bd.astype(BF16)
    inv_n = 1.0 / RW_HEAD

    def one_quad(rs, qd):
        cs = slice(qd * QUAD, (qd + 1) * QUAD)
        ln_w = par_ref[0:1, cs]
        ln_b = par_ref[1:2, cs]
        r_k = par_ref[2:3, cs]
        r = r_ref[rs, cs]
        lw = lw_ref[rs, cs]
        k = k_ref[rs, cs]
        v = v_ref[rs, cs]
        S = s_ref[qd]

        cw = _dot_rhs_split(tri, lw, 3)
        yield
        e_pos = jnp.exp(cw)
        e_neg = jnp.exp(-cw)
        at = a_ref[rs, cs] * jnp.exp(cw - lw)
        rt = r * e_pos
        bt = b_ref[rs, cs] * e_neg
        kt = k * e_neg
        wc = e_pos[C - 1:C, :]

        z = jnp.concatenate([_bdiag(bt, hm1), _bdiag(kt, hm1)], axis=0)
        g4 = _dot_nt(jnp.concatenate([at, rt], axis=0), z)
        yield
        a_ab = g4[0:C, 0:QUAD] * strict
        a_ak = g4[0:C, QUAD:2 * QUAD] * strict
        a_rb = g4[C:2 * C, 0:QUAD] * incl
        a_rk = g4[C:2 * C, QUAD:2 * QUAD] * incl

        d1 = a_ab * same_sub
        a_off = a_ab - d1
        v_bd = _bdiag(v, hm1)
        akv = _dot(a_ak, v_bd)
        d2 = _dot(d1, _bdiag(d1, hm1))
        yield
        d2_bd = _bdiag(d2, hm1)
        d4 = _dot(d2, d2_bd)
        p = eye_q + d1
        p = p + _dot(p, d2_bd)
        yield
        d4_bd = _bdiag(d4, hm1)
        d8 = _dot(d4, d4_bd)
        p = p + _dot(p, d4_bd)
        yield
        t_d = p + _dot(p, _bdiag(d8, hm1))
        yield
        nx = _dot(t_d, _bdiag(jnp.concatenate([a_off, at, akv], axis=1), hm3))
        yield
        n_m = nx[:, 0:QUAD].astype(BF16)
        cc = nx[:, QUAD:3 * QUAD]
        u = cc
        for _ in range(C // SUB - 1):
            u = cc + jnp.dot(n_m, _bdiag(u, hm2), preferred_element_type=F32)
            yield
        a_hat = u[:, 0:QUAD]
        u_hat = u[:, QUAD:2 * QUAD]

        m_c = (eye + _dot_tn(a_hat, bt) * bd) * wc
        n_c = (_dot_tn(u_hat, bt) + _dot_tn(v, kt)) * bd * wc
        q = rt + _dot(a_rb, _bdiag(a_hat, hm1))
        y0 = _dot(jnp.concatenate([a_rb, a_rk], axis=1),
                  jnp.concatenate([_bdiag(u_hat, hm1), v_bd], axis=0))
        yield
        y = _dot_nt(q, S) + y0
        s_ref[qd] = _dot(S, m_c) + n_c
        yield

        mean = _dot_lhs_split(y, bd16, 2) * inv_n
        bonus = _dot_lhs_split(r * k * r_k, bd16, 2) * v
        yield
        dlt = y - mean
        var = _dot_lhs_split(dlt * dlt, bd16, 2) * inv_n
        yield
        yn = dlt * lax.rsqrt(var + RWKV_GN_EPS) * ln_w + ln_b
        o_ref[rs, cs] = ((yn + bonus) * g_ref[rs, cs]).astype(o_ref.dtype)

    def chunk_body(ci, carry):
        rs = pl.ds(pl.multiple_of(ci * C, C), C)
        pending = [one_quad(rs, qd) for qd in range(n_quads)]
        while pending:
            pending = [g for g in pending if next(g, True) is None]
        return carry

    lax.fori_loop(0, n_chunks, chunk_body, 0)


def _rwkv_scan(r, lw, k, v, a, b, g, par, ls=256):
    T = r.shape[0]
    ls = min(ls, T)
    row = pl.BlockSpec((ls, RW_W), lambda i: (i, 0))
    return pl.pallas_call(
        functools.partial(_scan_kernel, n_chunks=ls // RCHUNK),
        grid=(T // ls,),
        in_specs=[row] * 7 + [pl.BlockSpec(par.shape, lambda i: (0, 0))],
        out_specs=row,
        out_shape=jax.ShapeDtypeStruct((T, RW_W), BF16),
        scratch_shapes=[pltpu.VMEM((RW_W // QUAD, QUAD, QUAD), F32)],
        compiler_params=_cparams(("arbitrary",)),
        name="rwkv_scan",
    )(r, lw, k, v, a, b, g, par)


def _outproj_kernel(x_ref, a_ref, y_ref, wa_ref, wy_ref, o_ref):
    o_ref[...] = (x_ref[...]
                  + jnp.dot(a_ref[...], wa_ref[...], preferred_element_type=F32)
                  + jnp.dot(y_ref[...], wy_ref[...], preferred_element_type=F32))


def _outproj(x, a_out, y, w_out, tm=1024, tn=512):
    T, D = x.shape
    tm = min(tm, T)
    return pl.pallas_call(
        _outproj_kernel,
        grid=(T // tm, D // tn),
        in_specs=[pl.BlockSpec((tm, tn), lambda i, j: (i, j)),
                  pl.BlockSpec((tm, ATT_W), lambda i, j: (i, 0)),
                  pl.BlockSpec((tm, RW_W), lambda i, j: (i, 0)),
                  pl.BlockSpec((ATT_W, tn), lambda i, j: (0, j)),
                  pl.BlockSpec((RW_W, tn), lambda i, j: (1, j))],
        out_specs=pl.BlockSpec((tm, tn), lambda i, j: (i, j)),
        out_shape=jax.ShapeDtypeStruct((T, D), F32),
        compiler_params=_cparams(("parallel", "arbitrary")),
        name="out_proj",
    )(x, a_out, y, w_out, w_out)


def _mlp_kernel(x_ref, g_ref, up_ref, dn_ref, o_ref, h_ref):
    f = pl.program_id(1)

    @pl.when(f == 0)
    def _():
        x = x_ref[...]
        ms = jnp.mean(x * x, axis=-1, keepdims=True)
        h_ref[...] = (x * lax.rsqrt(ms + RMS_EPS) * g_ref[...]).astype(BF16)
        o_ref[...] = x

    u = jnp.maximum(jnp.dot(h_ref[...], up_ref[...], preferred_element_type=F32), 0.0)
    o_ref[...] += jnp.dot((u * u).astype(BF16), dn_ref[...], preferred_element_type=F32)


def _mlp(x, g, up, down, tm=512, tf=512):
    T, D = x.shape
    F = up.shape[1]
    tm = min(tm, T)
    return pl.pallas_call(
        _mlp_kernel,
        grid=(T // tm, F // tf),
        in_specs=[pl.BlockSpec((tm, D), lambda i, f: (i, 0)),
                  pl.BlockSpec((1, D), lambda i, f: (0, 0)),
                  pl.BlockSpec((D, tf), lambda i, f: (0, f)),
                  pl.BlockSpec((tf, D), lambda i, f: (f, 0))],
        out_specs=pl.BlockSpec((tm, D), lambda i, f: (i, 0)),
        out_shape=jax.ShapeDtypeStruct((T, D), F32),
        scratch_shapes=[pltpu.VMEM((tm, D), BF16)],
        compiler_params=_cparams(("parallel", "arbitrary")),
        name="mlp",
    )(x, g.reshape(1, D), up, down)


def kernel(x, norm_mix, norm_mlp, w_in, w_out, qk_norm_q, qk_norm_k, diff_lambda_q, diff_lambda_k,
           diff_subln, rwkv_mu, rwkv_w0, rwkv_w2, rwkv_a0, rwkv_a2, rwkv_g2, rwkv_k_k, rwkv_k_a,
           rwkv_r_k, rwkv_ln_w, rwkv_ln_b, rwkv_v0, rwkv_v1, rwkv_v2, mlp_up, mlp_down):
    B, T, D = x.shape
    assert B == 1 and D == D_MODEL
    xs = x.reshape(T, D)
    n_qk = 2 * ATT_W
    slopes = jnp.asarray([2.0 ** (-8.0 * (i + 1) / N_HEADS_A) for i in range(N_HEADS_A)], F32)
    v_first = None
    for l in range(DEPTH):
        w = w_in[l]
        w_qk = w[:, :n_qk].astype(BF16)
        w_v = w[:, n_qk:n_qk + ATT_W].astype(BF16)
        w_rkv = w[:, n_qk + ATT_W:n_qk + ATT_W + 3 * RW_W].astype(BF16)
        w_lat = jnp.pad(w[:, n_qk + ATT_W + 3 * RW_W:], ((0, 0), (0, LAT_PAD - LAT_W))).astype(BF16)

        gq = jnp.tile(qk_norm_q[l].reshape(-1) * (LOG2E / math.sqrt(MAP_DIM)), N_HEADS_A)
        gk = jnp.tile(qk_norm_k[l].reshape(-1), N_HEADS_A)
        qk = _proj(xs, norm_mix[l], w_qk, BF16, gain=jnp.concatenate([gq, gk]))
        v_d = _proj(xs, norm_mix[l], w_v, BF16)
        rkv = _proj(xs, norm_mix[l], w_rkv, F32)
        lat = _proj(xs, norm_mix[l], w_lat, F32)

        lam_init = 0.8 - 0.6 * math.exp(-0.3 * l)
        lq, lk = diff_lambda_q[l], diff_lambda_k[l]
        lam = jnp.exp(jnp.sum(lq[0] * lk[0])) - jnp.exp(jnp.sum(lq[1] * lk[1])) + lam_init
        hp = jnp.zeros((N_HEADS_A, 8, HEAD_A), F32)
        hp = hp.at[:, 0, :].set((slopes * LOG2E)[:, None])
        hp = hp.at[:, 1, :].set(lam)
        hp = hp.at[:, 2, :].set(diff_subln[l][None, :] * (1.0 - lam_init))
        a_out = _attention(qk, v_d, hp)

        mu = rwkv_mu[l]
        mu_rkv = mu[:3 * RW_W].reshape(1, -1)
        mu_lat = jnp.pad(mu[3 * RW_W:], (0, LAT_PAD - LAT_W)).reshape(1, -1)
        has_gate = l > 0
        v0 = rwkv_v0[l - 1] if has_gate else jnp.zeros((RW_W,), F32)
        v1 = rwkv_v1[l - 1] if has_gate else jnp.zeros((RW_W, 32), F32)
        v2 = rwkv_v2[l - 1] if has_gate else jnp.zeros((32, RW_W), F32)
        vecs = jnp.zeros((8, RW_W), F32)
        for idx, vec in enumerate((rwkv_w0[l], rwkv_a0[l], rwkv_k_k[l], rwkv_k_a[l], v0)):
            vecs = vecs.at[idx].set(vec)
        vf = v_first if has_gate else rkv
        r, lw, k, v, a, b, g = _rwkv_prep(rkv, lat, vf, mu_rkv, mu_lat, vecs, rwkv_w2[l], rwkv_a2[l],
                                          rwkv_g2[l], v1, v2, has_gate)
        if l == 0:
            v_first = v
        par = jnp.zeros((8, RW_W), F32)
        par = par.at[0].set(rwkv_ln_w[l]).at[1].set(rwkv_ln_b[l]).at[2].set(rwkv_r_k[l].reshape(-1))
        y = _rwkv_scan(r, lw, k, v, a, b, g, par)

        xs = _outproj(xs, a_out, y, w_out[l].astype(BF16))
        xs = _mlp(xs, norm_mlp[l], mlp_up[l].astype(BF16), mlp_down[l].astype(BF16))
    return xs.reshape(B, T, D)
```

```python
import functools
import math

import jax
import jax.numpy as jnp
from jax import lax
from jax.experimental import pallas as pl
from jax.experimental.pallas import tpu as pltpu

F32 = jnp.float32
BF16 = jnp.bfloat16

D_MODEL = 2048
DEPTH = 4
CHUNK = 64
N_HEADS_A = 8
HEAD_A = 128
MAP_DIM = 64
ATT_W = N_HEADS_A * HEAD_A
RW_W = 1024
RW_HEAD = 64
DECAY_LORA = 64
AAA_LORA = 64
GATE_LORA = 160
LAT_W = DECAY_LORA + AAA_LORA + GATE_LORA
LAT_PAD = 384
D_FF = 4 * D_MODEL
RMS_EPS = 1e-6
RWKV_GN_EPS = 64e-5
NEG_BIG = -1e30
LOG2E = 1.4426950408889634

LANES = 128
QUAD = 4 * RW_HEAD
RCHUNK = 64
SUB = 16
VMEM_LIMIT = 56 * 1024 * 1024

HI = lax.Precision.HIGHEST


def _cparams(sem):
    return pltpu.CompilerParams(dimension_semantics=sem, vmem_limit_bytes=VMEM_LIMIT)


def _dot(a, b):
    return jnp.dot(a.astype(BF16), b.astype(BF16), preferred_element_type=F32)


def _dot_nt(a, b):
    return lax.dot_general(a.astype(BF16), b.astype(BF16), (((1,), (1,)), ((), ())),
                           preferred_element_type=F32)


def _dot_tn(a, b):
    return lax.dot_general(a.astype(BF16), b.astype(BF16), (((0,), (0,)), ((), ())),
                           preferred_element_type=F32)


def _dot_hi(a, b):
    return jnp.dot(a, b, precision=HI, preferred_element_type=F32)


def _group_ones(n, group):
    r = lax.broadcasted_iota(jnp.int32, (n, n), 0) // group
    c = lax.broadcasted_iota(jnp.int32, (n, n), 1) // group
    return (r == c).astype(F32)


def _proj_kernel(x_ref, g_ref, w_ref, *rest, qk):
    if qk:
        gain_ref, o_ref, h_ref = rest
    else:
        o_ref, h_ref = rest

    @pl.when(pl.program_id(1) == 0)
    def _():
        x = x_ref[...]
        ms = jnp.mean(x * x, axis=-1, keepdims=True)
        h_ref[...] = (x * lax.rsqrt(ms + RMS_EPS) * g_ref[...]).astype(BF16)

    acc = jnp.dot(h_ref[...], w_ref[...], preferred_element_type=F32)
    if not qk:
        o_ref[...] = acc.astype(o_ref.dtype)
        return
    tn = acc.shape[1]
    ones = _group_ones(LANES, MAP_DIM).astype(BF16)
    lane = lax.broadcasted_iota(jnp.int32, (1, LANES), 1)
    first = lane < MAP_DIM
    for s in range(tn // LANES):
        y = acc[:, s * LANES:(s + 1) * LANES]
        ss = jnp.dot((y * y).astype(BF16), ones, preferred_element_type=F32)
        yn = y * lax.rsqrt(ss * (1.0 / MAP_DIM) + RMS_EPS) * gain_ref[:, s * LANES:(s + 1) * LANES]
        o_ref[0, :, s * LANES:(s + 1) * LANES] = jnp.where(first, yn, 0.0).astype(o_ref.dtype)
        o_ref[1, :, s * LANES:(s + 1) * LANES] = jnp.where(
            first, pltpu.roll(yn, MAP_DIM, axis=1), 0.0).astype(o_ref.dtype)


def _proj(x, g, w, out_dtype, gain=None, tm=1024, tn=512):
    T, D = x.shape
    N = w.shape[1]
    tm = min(tm, T)
    tn = min(tn, N)
    if N % tn:
        tn = N
    qk = gain is not None
    in_specs = [pl.BlockSpec((tm, D), lambda i, j: (i, 0)),
                pl.BlockSpec((1, D), lambda i, j: (0, 0)),
                pl.BlockSpec((D, tn), lambda i, j: (0, j))]
    args = [x, g.reshape(1, D), w]
    if qk:
        in_specs.append(pl.BlockSpec((1, tn), lambda i, j: (0, j)))
        args.append(gain.reshape(1, N))
        out_shape = jax.ShapeDtypeStruct((2, T, N), out_dtype)
        out_spec = pl.BlockSpec((2, tm, tn), lambda i, j: (0, i, j))
    else:
        out_shape = jax.ShapeDtypeStruct((T, N), out_dtype)
        out_spec = pl.BlockSpec((tm, tn), lambda i, j: (i, j))
    return pl.pallas_call(
        functools.partial(_proj_kernel, qk=qk),
        grid=(T // tm, N // tn),
        in_specs=in_specs,
        out_specs=out_spec,
        out_shape=out_shape,
        scratch_shapes=[pltpu.VMEM((tm, D), BF16)],
        compiler_params=_cparams(("parallel", "arbitrary")),
        name="proj_qk" if qk else "proj",
    )(*args)


def _attn_kernel(reach_ref, q_ref, k_ref, v_ref, hp_ref, o_ref, s_ref, pm_ref, p_ref, al_ref, m_ref, acc_ref,
                 *, bq):
    i = pl.program_id(1)
    c = hp_ref[0, 0:1, 0:1]
    lam = hp_ref[0, 1:2, :]
    sub = hp_ref[0, 2:3, :]
    neg_inf = float("-inf")

    p_ref[1] = jnp.zeros(p_ref.shape[1:], BF16)
    al_ref[1] = jnp.ones(al_ref.shape[1:], F32)
    m_ref[...] = jnp.full(m_ref.shape, NEG_BIG, F32)
    acc_ref[...] = jnp.zeros(acc_ref.shape, F32)

    kj = lax.broadcasted_iota(jnp.int32, (1, bq), 1).astype(F32)
    ckj = c * kj
    ones_col = (lax.broadcasted_iota(jnp.int32, (bq, HEAD_A), 1) == 0).astype(BF16)

    def key_rows(step):
        return pl.ds(pl.multiple_of((i - jnp.maximum(step, 0)) * bq, bq), bq)

    def put_scores(slot, mp, s):
        s_ref[slot, mp] = s
        pm_ref[slot, mp] = functools.reduce(
            jnp.maximum, [s[:, n * LANES:(n + 1) * LANES] for n in range(bq // LANES)])

    def stage_c(step, mp):
        slot = step % 2
        vx = jnp.concatenate([v_ref[key_rows(step), :], ones_col], axis=1)
        acc_ref[mp] = al_ref[slot, mp] * acc_ref[mp] + jnp.dot(
            p_ref[slot, mp], vx, preferred_element_type=F32)

    def stage_b(step, mp):
        slot = step % 2
        m_old = m_ref[mp]
        m_new = jnp.maximum(m_old, jnp.max(pm_ref[slot, mp], axis=-1, keepdims=True))
        m_ref[mp] = m_new
        al_ref[slot, mp] = jnp.exp2(m_old - m_new)
        p_ref[slot, mp] = jnp.exp2(s_ref[slot, mp] - m_new).astype(BF16)

    def stage_a(step, mp):
        bias = ckj - c * (step * bq).astype(F32)
        put_scores(step % 2, mp, _dot_nt(q_ref[mp], k_ref[mp, key_rows(step), :]) + bias)

    qi = lax.broadcasted_iota(jnp.int32, (bq, bq), 0)
    kk = lax.broadcasted_iota(jnp.int32, (bq, bq), 1)
    allowed = (kk // CHUNK) <= (qi // CHUNK)
    dbias = jnp.where(allowed, c * jnp.minimum(kk, 2 * qi - kk).astype(F32), neg_inf)
    for mp in range(2):
        put_scores(0, mp, _dot_nt(q_ref[mp], k_ref[mp, key_rows(0), :]) + dbias)

    def body(t, carry):
        for mp in range(2):
            stage_c(t - 2, mp)
        for mp in range(2):
            stage_b(t - 1, mp)
        for mp in range(2):
            stage_a(t, mp)
        return carry

    n = jnp.minimum(i, reach_ref[pl.program_id(0)])
    lax.fori_loop(1, n + 1, body, 0)

    for mp in range(2):
        stage_c(n - 1, mp)
        stage_b(n, mp)
        stage_c(n, mp)

    a1 = acc_ref[0]
    a2 = acc_ref[1]
    o = (a1[:, :HEAD_A] / a1[:, HEAD_A:HEAD_A + 1]
         - lam * (a2[:, :HEAD_A] / a2[:, HEAD_A:HEAD_A + 1]))
    ms = jnp.mean(o * o, axis=-1, keepdims=True)
    o_ref[...] = (o * lax.rsqrt(ms + RMS_EPS) * sub).astype(o_ref.dtype)


ATT_BQ = 512


UNDERFLOW_LOG2 = 200.0
NORM_SLACK = 1.02


def _alibi_reach(gq, gk, slopes_log2e, bq):
    bound = MAP_DIM * NORM_SLACK ** 2 * jnp.max(jnp.abs(gq)) * jnp.max(jnp.abs(gk))
    x = (2.0 * bound + UNDERFLOW_LOG2) / (slopes_log2e * bq)
    return jnp.minimum(jnp.floor(x), 1e6).astype(jnp.int32) + 1


def _attention(qk, v, hp, reach):
    T = v.shape[0]
    bq = min(ATT_BQ, T)
    grid_spec = pltpu.PrefetchScalarGridSpec(
        num_scalar_prefetch=1,
        grid=(N_HEADS_A, T // bq),
        in_specs=[pl.BlockSpec((2, bq, HEAD_A), lambda h, i, r: (0, i, h)),
                  pl.BlockSpec((2, T, HEAD_A), lambda h, i, r: (0, 0, N_HEADS_A + h)),
                  pl.BlockSpec((T, HEAD_A), lambda h, i, r: (0, h)),
                  pl.BlockSpec((1, 8, HEAD_A), lambda h, i, r: (h, 0, 0))],
        out_specs=pl.BlockSpec((bq, HEAD_A), lambda h, i, r: (i, h)),
        scratch_shapes=[pltpu.VMEM((2, 2, bq, bq), F32),
                        pltpu.VMEM((2, 2, bq, LANES), F32),
                        pltpu.VMEM((2, 2, bq, bq), BF16),
                        pltpu.VMEM((2, 2, bq, 1), F32),
                        pltpu.VMEM((2, bq, 1), F32),
                        pltpu.VMEM((2, bq, 2 * HEAD_A), F32)])
    return pl.pallas_call(
        functools.partial(_attn_kernel, bq=bq),
        grid_spec=grid_spec,
        out_shape=jax.ShapeDtypeStruct((T, ATT_W), BF16),
        compiler_params=_cparams(("parallel", "arbitrary")),
        name="diff_attn",
    )(reach, qk, qk, v, hp)


def _shift(p, carry_row, mu):
    rolled = pltpu.roll(p, 1, axis=0)
    row = lax.broadcasted_iota(jnp.int32, p.shape, 0)
    prev = jnp.where(row == 0, carry_row, rolled)
    return p + (prev - p) * mu


def _prep_kernel(rkv_ref, lat_ref, vf_ref, mu_ref, mul_ref, vec_ref, w2_ref, a2_ref, g2_ref,
                 v1_ref, v2_ref,
                 r_ref, lw_ref, k_ref, v_ref, a_ref, b_ref, g_ref,
                 c_rkv, c_lat, *, has_gate):
    @pl.when(pl.program_id(0) == 0)
    def _():
        c_rkv[...] = jnp.zeros(c_rkv.shape, F32)
        c_lat[...] = jnp.zeros(c_lat.shape, F32)

    p = rkv_ref[...]
    lt = lat_ref[...]
    n = p.shape[0]
    ps = _shift(p, c_rkv[0:1, :], mu_ref[...])
    ls = _shift(lt, c_lat[0:1, :], mul_ref[...])
    c_rkv[0:1, :] = p[n - 1:n, :]
    c_lat[0:1, :] = lt[n - 1:n, :]

    w0, a0, k_k, k_a, v0 = (vec_ref[i:i + 1, :] for i in range(5))
    r = ps[:, 0:RW_W]
    kr = ps[:, RW_W:2 * RW_W]
    vr = ps[:, 2 * RW_W:3 * RW_W]
    lat_w = ls[:, 0:DECAY_LORA]
    lat_a = ls[:, DECAY_LORA:DECAY_LORA + AAA_LORA]
    lat_g = ls[:, DECAY_LORA + AAA_LORA:LAT_W]

    xw = w0 + _dot_hi(jnp.tanh(lat_w), w2_ref[...])
    w_log = -jax.nn.softplus(-xw) - 0.5
    lw_ref[...] = -jnp.exp(w_log)
    if has_gate:
        gate = jax.nn.sigmoid(v0 + _dot_hi(_dot_hi(vr, v1_ref[...]), v2_ref[...]))
        vr = vr + (vf_ref[...] - vr) * gate
    a_rate = jax.nn.sigmoid(a0 + _dot_hi(lat_a, a2_ref[...]))
    g_ref[...] = _dot_hi(jax.nn.sigmoid(lat_g), g2_ref[...])
    kk = kr * k_k
    ones = _group_ones(LANES, RW_HEAD)
    for s in range(RW_W // LANES):
        sl = slice(s * LANES, (s + 1) * LANES)
        ks = kk[:, sl]
        nrm = jnp.sqrt(_dot_hi(ks * ks, ones))
        kn = ks / jnp.maximum(nrm, 1e-12)
        a_ref[:, sl] = -kn
        b_ref[:, sl] = kn * a_rate[:, sl]
    r_ref[...] = r
    k_ref[...] = kr * (1.0 + (a_rate - 1.0) * k_a)
    v_ref[...] = vr


def _rwkv_prep(rkv, lat, v_first, mu_rkv, mu_lat, vecs, w2, a2, g2, v1, v2, has_gate, tp=256):
    T = rkv.shape[0]
    tp = min(tp, T)
    row = lambda w: pl.BlockSpec((tp, w), lambda i: (i, 0))
    full = lambda a: pl.BlockSpec(a.shape, lambda i: (0,) * a.ndim)
    outs = [jax.ShapeDtypeStruct((T, RW_W), F32)] * 7
    return pl.pallas_call(
        functools.partial(_prep_kernel, has_gate=has_gate),
        grid=(T // tp,),
        in_specs=[row(3 * RW_W), row(LAT_PAD), row(RW_W), full(mu_rkv), full(mu_lat), full(vecs),
                  full(w2), full(a2), full(g2), full(v1), full(v2)],
        out_specs=[row(RW_W)] * 7,
        out_shape=outs,
        scratch_shapes=[pltpu.VMEM((8, 3 * RW_W), F32), pltpu.VMEM((8, LAT_PAD), F32)],
        compiler_params=_cparams(("arbitrary",)),
        name="rwkv_prep",
    )(rkv, lat, v_first, mu_rkv, mu_lat, vecs, w2, a2, g2, v1, v2)


def _bdiag(y, masks):
    yb = y.astype(BF16)
    return jnp.concatenate([yb * m for m in masks], axis=0)


def _split(x, parts):
    pieces = []
    for _ in range(parts):
        h = x.astype(BF16)
        pieces.append(h)
        x = x - h.astype(F32)
    return pieces


def _dot_rhs_split(a, x, parts):
    n = x.shape[1]
    out = jnp.dot(a, jnp.concatenate(_split(x, parts), axis=1), preferred_element_type=F32)
    return sum(out[:, i * n:(i + 1) * n] for i in range(parts))


def _dot_lhs_split(x, a, parts):
    m = x.shape[0]
    out = jnp.dot(jnp.concatenate(_split(x, parts), axis=0), a, preferred_element_type=F32)
    return sum(out[i * m:(i + 1) * m, :] for i in range(parts))


def _scan_kernel(r_ref, lw_ref, k_ref, v_ref, a_ref, b_ref, g_ref, par_ref, o_ref, s_ref, *, n_chunks):
    C = RCHUNK
    n_quads = RW_W // QUAD

    @pl.when(pl.program_id(0) == 0)
    def _():
        s_ref[...] = jnp.zeros(s_ref.shape, F32)

    lane = lax.broadcasted_iota(jnp.int32, (1, QUAD), 1) // RW_HEAD
    hm1 = [(lane == q).astype(BF16) for q in range(4)]
    hm2 = [jnp.concatenate([m, m], axis=1) for m in hm1]
    hm3 = [jnp.concatenate([m, m, m], axis=1) for m in hm1]
    t_i = lax.broadcasted_iota(jnp.int32, (C, QUAD), 0)
    s_i = lax.broadcasted_iota(jnp.int32, (C, QUAD), 1) % C
    strict = (s_i < t_i).astype(F32)
    incl = (s_i <= t_i).astype(F32)
    same_sub = ((s_i // SUB) == (t_i // SUB)).astype(F32)
    eye_q = (s_i == t_i).astype(F32)
    bd = _group_ones(QUAD, RW_HEAD)
    eye = (lax.broadcasted_iota(jnp.int32, (QUAD, QUAD), 0)
           == lax.broadcasted_iota(jnp.int32, (QUAD, QUAD), 1)).astype(F32)
    tri = (lax.broadcasted_iota(jnp.int32, (C, C), 1)
           <= lax.broadcasted_iota(jnp.int32, (C, C), 0)).astype(BF16)
    bd16 = bd.astype(BF16)
    inv_n = 1.0 / RW_HEAD

    def one_quad(rs, qd):
        cs = slice(qd * QUAD, (qd + 1) * QUAD)
        ln_w = par_ref[0:1, cs]
        ln_b = par_ref[1:2, cs]
        r_k = par_ref[2:3, cs]
        r = r_ref[rs, cs]
        lw = lw_ref[rs, cs]
        k = k_ref[rs, cs]
        v = v_ref[rs, cs]
        S = s_ref[qd]

        cw = _dot_rhs_split(tri, lw, 3)
        yield
        e_pos = jnp.exp(cw)
        e_neg = jnp.exp(-cw)
        at = a_ref[rs, cs] * jnp.exp(cw - lw)
        rt = r * e_pos
        bt = b_ref[rs, cs] * e_neg
        kt = k * e_neg
        wc = e_pos[C - 1:C, :]

        z = jnp.concatenate([_bdiag(bt, hm1), _bdiag(kt, hm1)], axis=0)
        g4 = _dot_nt(jnp.concatenate([at, rt], axis=0), z)
        yield
        a_ab = g4[0:C, 0:QUAD] * strict
        a_ak = g4[0:C, QUAD:2 * QUAD] * strict
        a_rb = g4[C:2 * C, 0:QUAD] * incl
        a_rk = g4[C:2 * C, QUAD:2 * QUAD] * incl

        d1 = a_ab * same_sub
        a_off = a_ab - d1
        v_bd = _bdiag(v, hm1)
        akv = _dot(a_ak, v_bd)
        d2 = _dot(d1, _bdiag(d1, hm1))
        yield
        d2_bd = _bdiag(d2, hm1)
        d4 = _dot(d2, d2_bd)
        p = eye_q + d1
        p = p + _dot(p, d2_bd)
        yield
        d4_bd = _bdiag(d4, hm1)
        d8 = _dot(d4, d4_bd)
        p = p + _dot(p, d4_bd)
        yield
        t_d = p + _dot(p, _bdiag(d8, hm1))
        yield
        nx = _dot(t_d, _bdiag(jnp.concatenate([a_off, at, akv], axis=1), hm3))
        yield
        n_m = nx[:, 0:QUAD].astype(BF16)
        cc = nx[:, QUAD:3 * QUAD]
        u = cc
        for _ in range(C // SUB - 1):
            u = cc + jnp.dot(n_m, _bdiag(u, hm2), preferred_element_type=F32)
            yield
        a_hat = u[:, 0:QUAD]
        u_hat = u[:, QUAD:2 * QUAD]

        m_c = (eye + _dot_tn(a_hat, bt) * bd) * wc
        n_c = (_dot_tn(u_hat, bt) + _dot_tn(v, kt)) * bd * wc
        q = rt + _dot(a_rb, _bdiag(a_hat, hm1))
        y0 = _dot(jnp.concatenate([a_rb, a_rk], axis=1),
                  jnp.concatenate([_bdiag(u_hat, hm1), v_bd], axis=0))
        yield
        y = _dot_nt(q, S) + y0
        s_ref[qd] = _dot(S, m_c) + n_c
        yield

        mean = _dot_lhs_split(y, bd16, 2) * inv_n
        bonus = _dot_lhs_split(r * k * r_k, bd16, 2) * v
        yield
        dlt = y - mean
        var = _dot_lhs_split(dlt * dlt, bd16, 2) * inv_n
        yield
        yn = dlt * lax.rsqrt(var + RWKV_GN_EPS) * ln_w + ln_b
        o_ref[rs, cs] = ((yn + bonus) * g_ref[rs, cs]).astype(o_ref.dtype)

    def chunk_body(ci, carry):
        rs = pl.ds(pl.multiple_of(ci * C, C), C)
        pending = [one_quad(rs, qd) for qd in range(n_quads)]
        while pending:
            pending = [g for g in pending if next(g, True) is None]
        return carry

    lax.fori_loop(0, n_chunks, chunk_body, 0)


def _rwkv_scan(r, lw, k, v, a, b, g, par, ls=256):
    T = r.shape[0]
    ls = min(ls, T)
    row = pl.BlockSpec((ls, RW_W), lambda i: (i, 0))
    return pl.pallas_call(
        functools.partial(_scan_kernel, n_chunks=ls // RCHUNK),
        grid=(T // ls,),
        in_specs=[row] * 7 + [pl.BlockSpec(par.shape, lambda i: (0, 0))],
        out_specs=row,
        out_shape=jax.ShapeDtypeStruct((T, RW_W), BF16),
        scratch_shapes=[pltpu.VMEM((RW_W // QUAD, QUAD, QUAD), F32)],
        compiler_params=_cparams(("arbitrary",)),
        name="rwkv_scan",
    )(r, lw, k, v, a, b, g, par)


def _outproj_kernel(x_ref, a_ref, y_ref, wa_ref, wy_ref, o_ref):
    o_ref[...] = (x_ref[...]
                  + jnp.dot(a_ref[...], wa_ref[...], preferred_element_type=F32)
                  + jnp.dot(y_ref[...], wy_ref[...], preferred_element_type=F32))


def _outproj(x, a_out, y, w_out, tm=1024, tn=512):
    T, D = x.shape
    tm = min(tm, T)
    return pl.pallas_call(
        _outproj_kernel,
        grid=(T // tm, D // tn),
        in_specs=[pl.BlockSpec((tm, tn), lambda i, j: (i, j)),
                  pl.BlockSpec((tm, ATT_W), lambda i, j: (i, 0)),
                  pl.BlockSpec((tm, RW_W), lambda i, j: (i, 0)),
                  pl.BlockSpec((ATT_W, tn), lambda i, j: (0, j)),
                  pl.BlockSpec((RW_W, tn), lambda i, j: (1, j))],
        out_specs=pl.BlockSpec((tm, tn), lambda i, j: (i, j)),
        out_shape=jax.ShapeDtypeStruct((T, D), F32),
        compiler_params=_cparams(("parallel", "arbitrary")),
        name="out_proj",
    )(x, a_out, y, w_out, w_out)


def _mlp_kernel(x_ref, g_ref, up_ref, dn_ref, o_ref, h_ref):
    f = pl.program_id(1)

    @pl.when(f == 0)
    def _():
        x = x_ref[...]
        ms = jnp.mean(x * x, axis=-1, keepdims=True)
        h_ref[...] = (x * lax.rsqrt(ms + RMS_EPS) * g_ref[...]).astype(BF16)
        o_ref[...] = x

    u = jnp.maximum(jnp.dot(h_ref[...], up_ref[...], preferred_element_type=F32), 0.0)
    o_ref[...] += jnp.dot((u * u).astype(BF16), dn_ref[...], preferred_element_type=F32)


def _mlp(x, g, up, down, tm=512, tf=512):
    T, D = x.shape
    F = up.shape[1]
    tm = min(tm, T)
    return pl.pallas_call(
        _mlp_kernel,
        grid=(T // tm, F // tf),
        in_specs=[pl.BlockSpec((tm, D), lambda i, f: (i, 0)),
                  pl.BlockSpec((1, D), lambda i, f: (0, 0)),
                  pl.BlockSpec((D, tf), lambda i, f: (0, f)),
                  pl.BlockSpec((tf, D), lambda i, f: (f, 0))],
        out_specs=pl.BlockSpec((tm, D), lambda i, f: (i, 0)),
        out_shape=jax.ShapeDtypeStruct((T, D), F32),
        scratch_shapes=[pltpu.VMEM((tm, D), BF16)],
        compiler_params=_cparams(("parallel", "arbitrary")),
        name="mlp",
    )(x, g.reshape(1, D), up, down)


def kernel(x, norm_mix, norm_mlp, w_in, w_out, qk_norm_q, qk_norm_k, diff_lambda_q, diff_lambda_k,
           diff_subln, rwkv_mu, rwkv_w0, rwkv_w2, rwkv_a0, rwkv_a2, rwkv_g2, rwkv_k_k, rwkv_k_a,
           rwkv_r_k, rwkv_ln_w, rwkv_ln_b, rwkv_v0, rwkv_v1, rwkv_v2, mlp_up, mlp_down):
    B, T, D = x.shape
    assert B == 1 and D == D_MODEL
    xs = x.reshape(T, D)
    n_qk = 2 * ATT_W
    slopes = jnp.asarray([2.0 ** (-8.0 * (i + 1) / N_HEADS_A) for i in range(N_HEADS_A)], F32)
    v_first = None
    for l in range(DEPTH):
        w = w_in[l]
        w_qk = w[:, :n_qk].astype(BF16)
        w_v = w[:, n_qk:n_qk + ATT_W].astype(BF16)
        w_rkv = w[:, n_qk + ATT_W:n_qk + ATT_W + 3 * RW_W].astype(BF16)
        w_lat = jnp.pad(w[:, n_qk + ATT_W + 3 * RW_W:], ((0, 0), (0, LAT_PAD - LAT_W))).astype(BF16)

        gq = jnp.tile(qk_norm_q[l].reshape(-1) * (LOG2E / math.sqrt(MAP_DIM)), N_HEADS_A)
        gk = jnp.tile(qk_norm_k[l].reshape(-1), N_HEADS_A)
        qk = _proj(xs, norm_mix[l], w_qk, BF16, gain=jnp.concatenate([gq, gk]))
        v_d = _proj(xs, norm_mix[l], w_v, BF16)
        rkv = _proj(xs, norm_mix[l], w_rkv, F32)
        lat = _proj(xs, norm_mix[l], w_lat, F32)

        lam_init = 0.8 - 0.6 * math.exp(-0.3 * l)
        lq, lk = diff_lambda_q[l], diff_lambda_k[l]
        lam = jnp.exp(jnp.sum(lq[0] * lk[0])) - jnp.exp(jnp.sum(lq[1] * lk[1])) + lam_init
        hp = jnp.zeros((N_HEADS_A, 8, HEAD_A), F32)
        hp = hp.at[:, 0, :].set((slopes * LOG2E)[:, None])
        hp = hp.at[:, 1, :].set(lam)
        hp = hp.at[:, 2, :].set(diff_subln[l][None, :] * (1.0 - lam_init))
        reach = _alibi_reach(gq, gk, slopes * LOG2E, min(ATT_BQ, T))
        a_out = _attention(qk, v_d, hp, reach)

        mu = rwkv_mu[l]
        mu_rkv = mu[:3 * RW_W].reshape(1, -1)
        mu_lat = jnp.pad(mu[3 * RW_W:], (0, LAT_PAD - LAT_W)).reshape(1, -1)
        has_gate = l > 0
        v0 = rwkv_v0[l - 1] if has_gate else jnp.zeros((RW_W,), F32)
        v1 = rwkv_v1[l - 1] if has_gate else jnp.zeros((RW_W, 32), F32)
        v2 = rwkv_v2[l - 1] if has_gate else jnp.zeros((32, RW_W), F32)
        vecs = jnp.zeros((8, RW_W), F32)
        for idx, vec in enumerate((rwkv_w0[l], rwkv_a0[l], rwkv_k_k[l], rwkv_k_a[l], v0)):
            vecs = vecs.at[idx].set(vec)
        vf = v_first if has_gate else rkv
        r, lw, k, v, a, b, g = _rwkv_prep(rkv, lat, vf, mu_rkv, mu_lat, vecs, rwkv_w2[l], rwkv_a2[l],
                                          rwkv_g2[l], v1, v2, has_gate)
        if l == 0:
            v_first = v
        par = jnp.zeros((8, RW_W), F32)
        par = par.at[0].set(rwkv_ln_w[l]).at[1].set(rwkv_ln_b[l]).at[2].set(rwkv_r_k[l].reshape(-1))
        y = _rwkv_scan(r, lw, k, v, a, b, g, par)

        xs = _outproj(xs, a_out, y, w_out[l].astype(BF16))
        xs = _mlp(xs, norm_mlp[l], mlp_up[l].astype(BF16), mlp_down[l].astype(BF16))
    return xs.reshape(B, T, D)
```
